```python
import math
import jax, jax.numpy as jnp
from jax import lax
import numpy as np

D_MODEL = 1024
BATCH = 4
SEQ = 8192
DEPTH = 2
DEC_BATCH = 32
DEC_SEQ = 16
PAST_LEN = 4096

CHUNK = 64
N_EVEN = (DEPTH + 1) // 2
N_ODD = DEPTH // 2
CONV_CH = D_MODEL // 2
CONV_W = 31
GMLP_CH = D_MODEL // 2
GMLP_HEADS = 8
GMLP_HD = GMLP_CH // GMLP_HEADS
GMLP_CHUNK = 128
IN_AB = 2 * CONV_CH + 2 * GMLP_CH
MLA_HEADS = 8
Q_LORA = 256
KV_LORA = 256
QK_NOPE = 128
QK_ROPE = 64
V_HD = 128
ROPE_BASE = 10000.0
Q_BLOCK = 128
D_FF = 2816
N_EXPERTS = 8
TOP_K = 2
EXPERT_FF = 1408
EPS = 1e-6

kernel_name = "hybrid_streaming_encoder_step"


def rms_norm(x, g):
    xf = x.astype(jnp.float32)
    y = xf * lax.rsqrt(jnp.mean(xf * xf, axis=-1, keepdims=True) + EPS)
    return (y * g.astype(jnp.float32)).astype(x.dtype)


def layer_norm(x, g, b):
    xf = x.astype(jnp.float32)
    mu = jnp.mean(xf, axis=-1, keepdims=True)
    var = jnp.mean(jnp.square(xf - mu), axis=-1, keepdims=True)
    y = (xf - mu) * lax.rsqrt(var + EPS) * g.astype(jnp.float32) + b.astype(jnp.float32)
    return y.astype(x.dtype)


def adaln(c, w, b):
    m = jax.nn.silu(c) @ w + b
    return jnp.split(m[:, None, :], 6, axis=-1)


def modulate(x, g, shift, scale):
    return rms_norm(x, g) * (1 + scale) + shift


def causal_dwconv(a_full, w, b):
    y = lax.conv_general_dilated(a_full, w[:, None, :], window_strides=(1,), padding='VALID',
                                 dimension_numbers=('NWC', 'WIO', 'NWC'),
                                 feature_group_count=CONV_CH)
    return y + b


def gmlp_spatial_prompt(v, w_sp, b_sp):
    B, S, _ = v.shape
    n = S // GMLP_CHUNK
    w = jnp.tril(w_sp)
    vc = v.reshape(B, n, GMLP_CHUNK, GMLP_HEADS, GMLP_HD)
    s = jnp.einsum('hts,bnshd->bnthd', w, vc) + b_sp.T[:, :, None]
    return s.reshape(B, S, GMLP_CH)


def gmlp_spatial_sample(v, w_sp, b_sp):
    B, T, _ = v.shape
    w = jnp.tril(w_sp[:, :T, :T])
    s = jnp.einsum('hts,bshd->bthd', w, v.reshape(B, T, GMLP_HEADS, GMLP_HD)) + b_sp[:, :T].T[:, :, None]
    return s.reshape(B, T, GMLP_CH)


def mixer_ab(h, conv_hist, w_in, b_in, conv_w, conv_b, ln_a_g, ln_a_b, ln_v_g, ln_v_b,
             w_sp, b_sp, w_out, b_out, spatial):
    p = h @ w_in + b_in
    a = p[..., :CONV_CH] * jax.nn.sigmoid(p[..., CONV_CH:2 * CONV_CH])
    a_full = jnp.concatenate([conv_hist, a], axis=1)
    ya = jax.nn.silu(layer_norm(causal_dwconv(a_full, conv_w, conv_b), ln_a_g, ln_a_b))
    z = jax.nn.gelu(p[..., 2 * CONV_CH:])
    u = z[..., :GMLP_CH]
    v = layer_norm(z[..., GMLP_CH:], ln_v_g, ln_v_b)
    yb = u * spatial(v, w_sp, b_sp)
    out = jnp.concatenate([ya, yb], axis=-1) @ w_out + b_out
    return out, a_full[:, -(CONV_W - 1):], v


def rope_tables(pos):
    half = QK_ROPE // 2
    inv = jnp.exp(-math.log(ROPE_BASE) * jnp.arange(half, dtype=jnp.float32) / half)
    ang = pos.astype(jnp.float32)[:, None] * inv[None, :]
    return jnp.cos(ang), jnp.sin(ang)


def apply_rope(x, cos, sin):
    x1, x2 = jnp.split(x, 2, axis=-1)
    cos = cos.astype(x.dtype)
    sin = sin.astype(x.dtype)
    return jnp.concatenate([x1 * cos - x2 * sin, x1 * sin + x2 * cos], axis=-1)


def mla_project(h, w_dc, g_q, g_kv, w_uq, cos, sin):
    B, T, _ = h.shape
    p = h @ w_dc
    cq = rms_norm(p[..., :Q_LORA], g_q)
    ckv = rms_norm(p[..., Q_LORA:Q_LORA + KV_LORA], g_kv)
    k_pe = apply_rope(p[..., Q_LORA + KV_LORA:], cos, sin)
    q = (cq @ w_uq).reshape(B, T, MLA_HEADS, QK_NOPE + QK_ROPE)
    q_nope = q[..., :QK_NOPE]
    q_pe = apply_rope(q[..., QK_NOPE:], cos[:, None, :], sin[:, None, :])
    return q_nope, q_pe, ckv, k_pe


def mla_attend_prompt(q_nope, q_pe, ckv, k_pe, w_uk, w_uv):
    B, S, _ = ckv.shape
    k_nope = (ckv @ w_uk).reshape(B, S, MLA_HEADS, QK_NOPE)
    v = (ckv @ w_uv).reshape(B, S, MLA_HEADS, V_HD)
    nb = S // Q_BLOCK
    scale = (QK_NOPE + QK_ROPE) ** -0.5
    k_chunk = jnp.arange(S) // CHUNK
    qn = q_nope.reshape(B, nb, Q_BLOCK, MLA_HEADS, QK_NOPE).swapaxes(0, 1)
    qp = q_pe.reshape(B, nb, Q_BLOCK, MLA_HEADS, QK_ROPE).swapaxes(0, 1)

    def block(args):
        i, qn_b, qp_b = args
        s = (jnp.einsum('bqhd,bkhd->bhqk', qn_b, k_nope)
             + jnp.einsum('bqhr,bkr->bhqk', qp_b, k_pe)).astype(jnp.float32) * scale
        q_chunk = (i * Q_BLOCK + jnp.arange(Q_BLOCK)) // CHUNK
        mask = q_chunk[:, None] >= k_chunk[None, :]
        s = jnp.where(mask, s, jnp.finfo(jnp.float32).min)
        pr = jax.nn.softmax(s, axis=-1).astype(v.dtype)
        return jnp.einsum('bhqk,bkhv->bqhv', pr, v)

    o = lax.map(block, (jnp.arange(nb), qn, qp))
    return o.swapaxes(0, 1).reshape(B, S, MLA_HEADS * V_HD)


def mla_attend_sample(q_nope, q_pe, ckv_all, kpe_all, w_uk, w_uv):
    B, T = q_nope.shape[:2]
    scale = (QK_NOPE + QK_ROPE) ** -0.5
    wk = w_uk.reshape(KV_LORA, MLA_HEADS, QK_NOPE)
    wv = w_uv.reshape(KV_LORA, MLA_HEADS, V_HD)
    q_lat = jnp.einsum('bthn,khn->bthk', q_nope, wk)
    s = (jnp.einsum('bthk,bsk->bhts', q_lat, ckv_all)
         + jnp.einsum('bthr,bsr->bhts', q_pe, kpe_all)).astype(jnp.float32) * scale
    pr = jax.nn.softmax(s, axis=-1).astype(ckv_all.dtype)
    o_lat = jnp.einsum('bhts,bsk->bthk', pr, ckv_all)
    o = jnp.einsum('bthk,khv->bthv', o_lat, wv)
    return o.reshape(B, T, MLA_HEADS * V_HD)


def swiglu(h, w_gu, w_down):
    g, u = jnp.split(h @ w_gu, 2, axis=-1)
    return (jax.nn.silu(g) * u) @ w_down


def moe_swiglu(h, w_router, b_router, w_exp_gu, w_exp_down):
    logits = (h @ w_router + b_router).astype(jnp.float32)
    top_v, top_i = lax.top_k(logits, TOP_K)
    gates = jax.nn.softmax(top_v, axis=-1)
    combine = jnp.sum(jax.nn.one_hot(top_i, N_EXPERTS, dtype=jnp.float32) * gates[..., None],
                      axis=-2).astype(h.dtype)
    out = jnp.zeros_like(h)
    for e in range(N_EXPERTS):
        out = out + combine[..., e:e + 1] * swiglu(h, w_exp_gu[e], w_exp_down[e])
    return out


def setup_inputs(seed: int = 0) -> dict:
    key = jax.random.key(seed)
    ks = iter(jax.random.split(key, 48))

    def nrm(shape, scale):
        return jax.random.normal(next(ks), shape, jnp.float32) * scale

    D = D_MODEL
    H = MLA_HEADS
    return {
        "x_prompt": nrm((BATCH, SEQ, D), 1.0),
        "x_sample": nrm((DEC_BATCH, DEC_SEQ, D), 1.0),
        "c_prompt": nrm((BATCH, D), 1.0),
        "c_sample": nrm((DEC_BATCH, D), 1.0),
        "state_conv": nrm((N_EVEN, DEC_BATCH, CONV_W - 1, CONV_CH), 0.5),
        "cache_ckv": nrm((N_ODD, DEC_BATCH, PAST_LEN, KV_LORA), 1.0),
        "cache_kpe": nrm((N_ODD, DEC_BATCH, PAST_LEN, QK_ROPE), 1.0),
        "norm1_g": 1.0 + nrm((DEPTH, D), 0.02),
        "norm2_g": 1.0 + nrm((DEPTH, D), 0.02),
        "w_ada": nrm((DEPTH, D, 6 * D), D ** -0.5),
        "b_ada": nrm((DEPTH, 6 * D), 0.02),
        "w_in_ab": nrm((N_EVEN, D, IN_AB), D ** -0.5),
        "b_in_ab": nrm((N_EVEN, IN_AB), 0.02),
        "conv_w": nrm((N_EVEN, CONV_W, CONV_CH), CONV_W ** -0.5),
        "conv_b": nrm((N_EVEN, CONV_CH), 0.02),
        "ln_conv_g": 1.0 + nrm((N_EVEN, CONV_CH), 0.02),
        "ln_conv_b": nrm((N_EVEN, CONV_CH), 0.02),
        "ln_v_g": 1.0 + nrm((N_EVEN, GMLP_CH), 0.02),
        "ln_v_b": nrm((N_EVEN, GMLP_CH), 0.02),
        "w_spatial": nrm((N_EVEN, GMLP_HEADS, GMLP_CHUNK, GMLP_CHUNK), GMLP_CHUNK ** -0.5),
        "b_spatial": 1.0 + nrm((N_EVEN, GMLP_HEADS, GMLP_CHUNK), 0.02),
        "w_out_ab": nrm((N_EVEN, CONV_CH + GMLP_CH, D), (CONV_CH + GMLP_CH) ** -0.5),
        "b_out_ab": nrm((N_EVEN, D), 0.02),
        "w_ffn_gu": nrm((N_EVEN, D, 2 * D_FF), D ** -0.5),
        "w_ffn_down": nrm((N_EVEN, D_FF, D), D_FF ** -0.5),
        "w_dc": nrm((N_ODD, D, Q_LORA + KV_LORA + QK_ROPE), D ** -0.5),
        "g_q": 1.0 + nrm((N_ODD, Q_LORA), 0.02),
        "g_kv": 1.0 + nrm((N_ODD, KV_LORA), 0.02),
        "w_uq": nrm((N_ODD, Q_LORA, H * (QK_NOPE + QK_ROPE)), Q_LORA ** -0.5),
        "w_uk": nrm((N_ODD, KV_LORA, H * QK_NOPE), KV_LORA ** -0.5),
        "w_uv": nrm((N_ODD, KV_LORA, H * V_HD), KV_LORA ** -0.5),
        "w_o": nrm((N_ODD, H * V_HD, D), (H * V_HD) ** -0.5),
        "w_router": nrm((N_ODD, D, N_EXPERTS), D ** -0.5),
        "b_router": nrm((N_ODD, N_EXPERTS), 0.01),
        "w_exp_gu": nrm((N_ODD, N_EXPERTS, D, 2 * EXPERT_FF), D ** -0.5),
        "w_exp_down": nrm((N_ODD, N_EXPERTS, EXPERT_FF, D), EXPERT_FF ** -0.5),
        "final_g": 1.0 + nrm((D,), 0.02),
    }


def reference(x_prompt, x_sample, c_prompt, c_sample, state_conv, cache_ckv, cache_kpe,
              norm1_g, norm2_g, w_ada, b_ada,
              w_in_ab, b_in_ab, conv_w, conv_b, ln_conv_g, ln_conv_b, ln_v_g, ln_v_b,
              w_spatial, b_spatial, w_out_ab, b_out_ab, w_ffn_gu, w_ffn_down,
              w_dc, g_q, g_kv, w_uq, w_uk, w_uv, w_o,
              w_router, b_router, w_exp_gu, w_exp_down, final_g):
    S = x_prompt.shape[1]
    T = x_sample.shape[1]
    cos_p, sin_p = rope_tables(jnp.arange(S))
    cos_s, sin_s = rope_tables(PAST_LEN + jnp.arange(T))
    xp, xs = x_prompt, x_sample
    conv_p, conv_s, gv_s, ckv_p, kpe_p, ckv_s, kpe_s = [], [], [], [], [], [], []
    for l in range(DEPTH):
        j = l // 2
        mp = adaln(c_prompt, w_ada[l], b_ada[l])
        ms = adaln(c_sample, w_ada[l], b_ada[l])
        hp = modulate(xp, norm1_g[l], mp[0], mp[1])
        hs = modulate(xs, norm1_g[l], ms[0], ms[1])
        if l % 2 == 0:
            ab = (w_in_ab[j], b_in_ab[j], conv_w[j], conv_b[j], ln_conv_g[j], ln_conv_b[j],
                  ln_v_g[j], ln_v_b[j], w_spatial[j], b_spatial[j], w_out_ab[j], b_out_ab[j])
            zero_hist = jnp.zeros((hp.shape[0], CONV_W - 1, CONV_CH), hp.dtype)
            op, cp, _ = mixer_ab(hp, zero_hist, *ab, gmlp_spatial_prompt)
            os_, cs, vs = mixer_ab(hs, state_conv[j], *ab, gmlp_spatial_sample)
            conv_p.append(cp)
            conv_s.append(cs)
            gv_s.append(vs)
        else:
            qn, qr, ckv_new, kpe_new = mla_project(hp, w_dc[j], g_q[j], g_kv[j], w_uq[j], cos_p, sin_p)
            op = mla_attend_prompt(qn, qr, ckv_new, kpe_new, w_uk[j], w_uv[j]) @ w_o[j]
            ckv_p.append(ckv_new)
            kpe_p.append(kpe_new)
            qn, qr, ckv_new, kpe_new = mla_project(hs, w_dc[j], g_q[j], g_kv[j], w_uq[j], cos_s, sin_s)
            ckv_all = jnp.concatenate([cache_ckv[j], ckv_new], axis=1)
            kpe_all = jnp.concatenate([cache_kpe[j], kpe_new], axis=1)
            os_ = mla_attend_sample(qn, qr, ckv_all, kpe_all, w_uk[j], w_uv[j]) @ w_o[j]
            ckv_s.append(ckv_new)
            kpe_s.append(kpe_new)
        xp = xp + mp[2] * op
        xs = xs + ms[2] * os_
        hp = modulate(xp, norm2_g[l], mp[3], mp[4])
        hs = modulate(xs, norm2_g[l], ms[3], ms[4])
        if l % 2 == 0:
            fp = swiglu(hp, w_ffn_gu[j], w_ffn_down[j])
            fs = swiglu(hs, w_ffn_gu[j], w_ffn_down[j])
        else:
            fp = moe_swiglu(hp, w_router[j], b_router[j], w_exp_gu[j], w_exp_down[j])
            fs = moe_swiglu(hs, w_router[j], b_router[j], w_exp_gu[j], w_exp_down[j])
        xp = xp + mp[5] * fp
        xs = xs + ms[5] * fs
    y_prompt = rms_norm(xp, final_g)
    y_sample = rms_norm(xs, final_g)
    return (y_prompt, y_sample, jnp.stack(conv_p), jnp.stack(conv_s), jnp.stack(gv_s),
            jnp.stack(ckv_p), jnp.stack(kpe_p), jnp.stack(ckv_s), jnp.stack(kpe_s))
```

```python
import functools
import math

import jax
import jax.numpy as jnp
from jax import lax
from jax.experimental import pallas as pl
from jax.experimental.pallas import tpu as pltpu

F32 = jnp.float32
BF16 = jnp.bfloat16

EPS = 1e-6
CHUNK = 64
CONV_W = 31
HIST = CONV_W - 1
HIST_PAD = 32
GMLP_CHUNK = 128
GMLP_HEADS = 8
MLA_HEADS = 8
Q_LORA = 256
KV_LORA = 256
QK_NOPE = 128
QK_ROPE = 64
V_HD = 128
QK_CAT = 256
ROPE_BASE = 10000.0
N_EXPERTS = 8
LANES = 128
NEG_BIG = -1e30

VMEM_LIMIT = 56 * 1024 * 1024


def _cparams(*sem):
    return pltpu.CompilerParams(dimension_semantics=sem, vmem_limit_bytes=VMEM_LIMIT)


def _const_spec(shape):
    nd = len(shape)
    return pl.BlockSpec(shape, lambda *_: (0,) * nd, pipeline_mode=pl.Buffered(1))


def _modulate(x, g, shift, scale):
    y = x * lax.rsqrt(jnp.mean(x * x, axis=-1, keepdims=True) + EPS)
    return (y * g) * (1.0 + scale) + shift


def _layer_norm(x, g, b):
    mu = jnp.mean(x, axis=-1, keepdims=True)
    xc = x - mu
    var = jnp.mean(xc * xc, axis=-1, keepdims=True)
    return xc * lax.rsqrt(var + EPS) * g + b


def _dot(a, b):
    return jnp.dot(a, b, preferred_element_type=F32)


def _adaln_kernel(c_ref, w_ref, b_ref, o_ref):
    c = c_ref[...]
    a = jax.nn.silu(c).astype(BF16)
    o_ref[...] = _dot(a, w_ref[...].astype(BF16)) + b_ref[...]


def _adaln(c_all, w_ada, b_ada, tn=1536):
    depth, d, n = w_ada.shape
    rows = c_all.shape[0]
    return pl.pallas_call(
        _adaln_kernel,
        grid=(depth, n // tn),
        in_specs=[
            pl.BlockSpec((rows, d), lambda l, j: (0, 0)),
            pl.BlockSpec((None, d, tn), lambda l, j: (l, 0, j)),
            pl.BlockSpec((None, 1, tn), lambda l, j: (l, 0, j)),
        ],
        out_specs=pl.BlockSpec((None, rows, tn), lambda l, j: (l, 0, j)),
        out_shape=jax.ShapeDtypeStruct((depth, rows, n), F32),
        compiler_params=_cparams("parallel", "parallel"),
        name="adaln",
    )(c_all, w_ada, b_ada.reshape(depth, 1, n))


def _mixer_kernel(*refs, ns, t, carry_hist, conv_rows):
    if carry_hist:
        (x_ref, mod_ref, g_ref, win_ref, bin_ref, cw_ref, cb_ref, lag_ref, lab_ref,
         lvg_ref, lvb_ref, wsp_ref, bsp_ref, woa_ref, wob_ref, bout_ref,
         xo_ref, conv_ref, aext_ref, ya_ref, yb_ref, tapwin_ref) = refs
        hist_ref = v_ref = None
    else:
        (x_ref, mod_ref, g_ref, win_ref, bin_ref, cw_ref, cb_ref, lag_ref, lab_ref,
         lvg_ref, lvb_ref, wsp_ref, bsp_ref, woa_ref, wob_ref, bout_ref, hist_ref,
         xo_ref, conv_ref, v_ref, aext_ref, ya_ref, yb_ref, tapwin_ref) = refs
    c = cw_ref.shape[1]
    rows = ns * t

    x = x_ref[...]
    h = _modulate(x, g_ref[...], mod_ref[0], mod_ref[1]).astype(BF16)
    p = _dot(h, win_ref[...]) + bin_ref[...]

    a = p[:, :c] * jax.nn.sigmoid(p[:, c:2 * c])
    if carry_hist:
        @pl.when(pl.program_id(1) == 0)
        def _():
            aext_ref[:, :HIST_PAD, :] = jnp.zeros((ns, HIST_PAD, c), F32)
    else:
        aext_ref[:, HIST_PAD - HIST:HIST_PAD, :] = hist_ref[...]
    aext_ref[:, HIST_PAD:, :] = a.reshape(ns, t, c)

    off = HIST_PAD - HIST
    n_cc = t // conv_rows
    lag = lag_ref[...]
    lab = lab_ref[...]
    cb = cb_ref[...]

    def conv_block(src_ref, s):
        acc = jnp.zeros((conv_rows, c), F32)
        for k in range(CONV_W):
            acc = acc + src_ref[s, off + k:off + k + conv_rows, :] * cw_ref[k:k + 1, :]
        y = _layer_norm(acc + cb, lag, lab)
        return jax.nn.silu(y).astype(BF16)

    for s in range(ns):
        if n_cc == 1:
            ya_ref[s * t:(s + 1) * t, :] = conv_block(aext_ref, s)
        else:
            def body(i, carry, s=s):
                r0 = pl.multiple_of(i * conv_rows, conv_rows)
                tapwin_ref[0] = aext_ref[s, pl.ds(r0, conv_rows + HIST_PAD), :]
                ya_ref[pl.ds(s * t + r0, conv_rows), :] = conv_block(tapwin_ref, 0)
                return carry
            lax.fori_loop(0, n_cc, body, 0)

    conv_ref[...] = aext_ref[:, t + HIST_PAD - HIST:t + HIST_PAD, :]
    if carry_hist:
        aext_ref[:, :HIST_PAD, :] = aext_ref[:, t:t + HIST_PAD, :]

    z = jax.nn.gelu(p[:, 2 * c:])
    u = z[:, :c]
    v = _layer_norm(z[:, c:], lvg_ref[...], lvb_ref[...])
    if v_ref is not None:
        v_ref[...] = v
    ri = lax.broadcasted_iota(jnp.int32, (GMLP_CHUNK, GMLP_CHUNK), 0)
    ci = lax.broadcasted_iota(jnp.int32, (GMLP_CHUNK, GMLP_CHUNK), 1)
    tril = ri >= ci
    low_half = ci < (LANES // 2)
    w_heads = [jnp.where(tril, wsp_ref[hh], 0.0).astype(BF16) for hh in range(GMLP_HEADS)]
    for blk in range(rows // GMLP_CHUNK):
        r = slice(blk * GMLP_CHUNK, (blk + 1) * GMLP_CHUNK)
        for j in range(c // LANES):
            l = slice(j * LANES, (j + 1) * LANES)
            vp = v[r, l]
            lo = jnp.where(low_half, vp, 0.0).astype(BF16)
            hi = jnp.where(low_half, 0.0, vp).astype(BF16)
            sp = _dot(w_heads[2 * j], lo) + _dot(w_heads[2 * j + 1], hi)
            yb_ref[r, l] = (u[r, l] * (sp + bsp_ref[:, l])).astype(BF16)

    out = _dot(ya_ref[...], woa_ref[...]) + _dot(yb_ref[...], wob_ref[...]) + bout_ref[...]
    xo_ref[...] = x + mod_ref[2] * out


def _mixer(x, mod, mod_spec, g1, wts, *, ns, t, n_outer, n_inner, hist=None):
    (w_in, b_in, cw, cb, lag, lab, lvg, lvb, wsp, bsp, woa, wob, b_out) = wts
    d = x.shape[1]
    c = cw.shape[1]
    rows = ns * t
    carry_hist = hist is None
    tile = lambda o, i: (o * n_inner + i, 0)
    in_specs = [
        pl.BlockSpec((rows, d), tile),
        mod_spec,
        _const_spec((1, d)),
        _const_spec(w_in.shape), _const_spec(b_in.shape),
        _const_spec(cw.shape), _const_spec(cb.shape),
        _const_spec(lag.shape), _const_spec(lab.shape),
        _const_spec(lvg.shape), _const_spec(lvb.shape),
        _const_spec(wsp.shape), _const_spec(bsp.shape),
        _const_spec(woa.shape), _const_spec(wob.shape), _const_spec(b_out.shape),
    ]
    args = [x, mod, g1, w_in, b_in, cw, cb, lag, lab, lvg, lvb, wsp, bsp, woa, wob, b_out]
    n_tiles = n_outer * n_inner
    out_shape = [jax.ShapeDtypeStruct(x.shape, F32)]
    out_specs = [pl.BlockSpec((rows, d), tile)]
    if carry_hist:
        out_shape.append(jax.ShapeDtypeStruct((n_outer * ns, HIST, c), F32))
        out_specs.append(pl.BlockSpec((ns, HIST, c), lambda o, i: (o, 0, 0)))
    else:
        in_specs.append(pl.BlockSpec((ns, HIST, c), lambda o, i: (o * n_inner + i, 0, 0)))
        args.append(hist)
        out_shape.append(jax.ShapeDtypeStruct((n_tiles * ns, HIST, c), F32))
        out_specs.append(pl.BlockSpec((ns, HIST, c), lambda o, i: (o * n_inner + i, 0, 0)))
        out_shape.append(jax.ShapeDtypeStruct((x.shape[0], c), F32))
        out_specs.append(pl.BlockSpec((rows, c), tile))
    conv_rows = min(t, 64)
    return pl.pallas_call(
        functools.partial(_mixer_kernel, ns=ns, t=t, carry_hist=carry_hist, conv_rows=conv_rows),
        grid=(n_outer, n_inner),
        in_specs=in_specs,
        out_specs=out_specs,
        out_shape=out_shape,
        scratch_shapes=[
            pltpu.VMEM((ns, t + HIST_PAD, c), F32),
            pltpu.VMEM((rows, c), BF16),
            pltpu.VMEM((rows, c), BF16),
            pltpu.VMEM((1, conv_rows + HIST_PAD, c), F32),
        ],
        compiler_params=_cparams("arbitrary", "arbitrary"),
        name="mixer_prompt" if carry_hist else "mixer_sample",
    )(*args)


def _ffn_kernel(x_ref, mod_ref, g_ref, wg_ref, wu_ref, wd_ref, o_ref, *, n_chunks):
    x = x_ref[...]
    h = _modulate(x, g_ref[...], mod_ref[0], mod_ref[1]).astype(BF16)
    ff = wg_ref.shape[1]
    cw = ff // n_chunks
    acc = None
    for j in range(n_chunks):
        sl = slice(j * cw, (j + 1) * cw)
        gate = _dot(h, wg_ref[:, sl])
        up = _dot(h, wu_ref[:, sl])
        act = (jax.nn.silu(gate) * up).astype(BF16)
        part = _dot(act, wd_ref[sl, :])
        acc = part if acc is None else acc + part
    o_ref[...] = x + mod_ref[2] * acc


def _ffn(x, mod, mod_spec, g2, wg, wu, wd, *, tm):
    n, d = x.shape
    return pl.pallas_call(
        functools.partial(_ffn_kernel, n_chunks=2),
        grid=(n // tm,),
        in_specs=[
            pl.BlockSpec((tm, d), lambda i: (i, 0)),
            mod_spec,
            _const_spec((1, d)),
            _const_spec(wg.shape), _const_spec(wu.shape), _const_spec(wd.shape),
        ],
        out_specs=pl.BlockSpec((tm, d), lambda i: (i, 0)),
        out_shape=jax.ShapeDtypeStruct((n, d), F32),
        compiler_params=_cparams("parallel"),
        name="ffn",
    )(x, mod, g2, wg, wu, wd)


def _mla_latents(x_ref, mod_ref, g_ref, wdc_ref, gq_ref, gkv_ref, tc, ts):
    x = x_ref[...]
    h = _modulate(x, g_ref[...], mod_ref[0], mod_ref[1]).astype(BF16)
    p = _dot(h, wdc_ref[...])
    cq = p[:, :Q_LORA]
    cq = cq * lax.rsqrt(jnp.mean(cq * cq, axis=-1, keepdims=True) + EPS) * gq_ref[...]
    ckv = p[:, Q_LORA:Q_LORA + KV_LORA]
    ckv = ckv * lax.rsqrt(jnp.mean(ckv * ckv, axis=-1, keepdims=True) + EPS) * gkv_ref[...]
    b0 = Q_LORA + KV_LORA
    kpe = p[:, b0:b0 + LANES] * tc + p[:, b0 + LANES:b0 + 2 * LANES] * ts
    return cq.astype(BF16), ckv, kpe


def _mla_prompt_kernel(x_ref, mod_ref, g_ref, wdc_ref, gq_ref, gkv_ref, tc_ref, ts_ref,
                       wqa_ref, wqb_ref, wuk_ref, wuv_ref,
                       ckv_ref, kpe_ref, q_ref, k_ref, v_ref):
    tc = tc_ref[...]
    ts = ts_ref[...]
    cq, ckv, kpe = _mla_latents(x_ref, mod_ref, g_ref, wdc_ref, gq_ref, gkv_ref, tc, ts)
    ckv_ref[...] = ckv
    kpe_ref[...] = kpe[:, :QK_ROPE]
    qa = _dot(cq, wqa_ref[...])
    qb = _dot(cq, wqb_ref[...])
    ckv_b = ckv.astype(BF16)
    kn = _dot(ckv_b, wuk_ref[...])
    v_ref[...] = _dot(ckv_b, wuv_ref[...]).astype(BF16)
    kpe_b = kpe.astype(BF16)
    for hh in range(MLA_HEADS):
        base = hh * QK_CAT
        q_ref[:, base:base + QK_NOPE] = qa[:, base:base + QK_NOPE].astype(BF16)
        q_ref[:, base + QK_NOPE:base + QK_CAT] = (
            qa[:, base + QK_NOPE:base + QK_CAT] * tc
            + qb[:, hh * LANES:(hh + 1) * LANES] * ts).astype(BF16)
        k_ref[:, base:base + QK_NOPE] = kn[:, hh * QK_NOPE:(hh + 1) * QK_NOPE].astype(BF16)
        k_ref[:, base + QK_NOPE:base + QK_CAT] = kpe_b


def _mla_prompt(x, mod, mod_spec, g1, wts, tc, ts, *, tm, tiles_per_seq):
    wdc, gq, gkv, wqa, wqb, wuk, wuv = wts
    n, d = x.shape
    hq = MLA_HEADS * QK_CAT
    row = lambda i: (i, 0)
    pos = lambda i: (i % tiles_per_seq, 0)
    return pl.pallas_call(
        _mla_prompt_kernel,
        grid=(n // tm,),
        in_specs=[
            pl.BlockSpec((tm, d), row), mod_spec, _const_spec((1, d)),
            _const_spec(wdc.shape), _const_spec(gq.shape), _const_spec(gkv.shape),
            pl.BlockSpec((tm, LANES), pos), pl.BlockSpec((tm, LANES), pos),
            _const_spec(wqa.shape), _const_spec(wqb.shape),
            _const_spec(wuk.shape), _const_spec(wuv.shape),
        ],
        out_specs=[
            pl.BlockSpec((tm, KV_LORA), row), pl.BlockSpec((tm, QK_ROPE), row),
            pl.BlockSpec((tm, hq), row), pl.BlockSpec((tm, hq), row),
            pl.BlockSpec((tm, MLA_HEADS * V_HD), row),
        ],
        out_shape=[
            jax.ShapeDtypeStruct((n, KV_LORA), F32), jax.ShapeDtypeStruct((n, QK_ROPE), F32),
            jax.ShapeDtypeStruct((n, hq), BF16), jax.ShapeDtypeStruct((n, hq), BF16),
            jax.ShapeDtypeStruct((n, MLA_HEADS * V_HD), BF16),
        ],
        compiler_params=_cparams("parallel"),
        name="mla_proj_prompt",
    )(x, mod, g1, wdc, gq, gkv, tc, ts, wqa, wqb, wuk, wuv)


def _mla_sample_kernel(x_ref, mod_ref, g_ref, wdc_ref, gq_ref, gkv_ref, tc_ref, ts_ref,
                       wqn_ref, wqp_ref, wqs_ref, wukt_ref,
                       ckv_ref, kpe_ref, qlat_ref, qpe_ref):
    tc = tc_ref[...]
    ts = ts_ref[...]
    cq, ckv, kpe = _mla_latents(x_ref, mod_ref, g_ref, wdc_ref, gq_ref, gkv_ref, tc, ts)
    ckv_ref[...] = ckv
    kpe_ref[...] = kpe[:, :QK_ROPE]
    qn = _dot(cq, wqn_ref[...]).astype(BF16)
    qp = _dot(cq, wqp_ref[...])
    qs = _dot(cq, wqs_ref[...])
    for j in range(MLA_HEADS * QK_ROPE // LANES):
        l = slice(j * LANES, (j + 1) * LANES)
        qpe_ref[:, l] = (qp[:, l] * tc + qs[:, l] * ts).astype(BF16)
    for hh in range(MLA_HEADS):
        qlat_ref[:, hh * KV_LORA:(hh + 1) * KV_LORA] = _dot(
            qn[:, hh * QK_NOPE:(hh + 1) * QK_NOPE], wukt_ref[hh]).astype(BF16)


def _mla_sample(x, mod, mod_spec, g1, wts, tc, ts, *, tm):
    wdc, gq, gkv, wqn, wqp, wqs, wukt = wts
    n, d = x.shape
    row = lambda i: (i, 0)
    return pl.pallas_call(
        _mla_sample_kernel,
        grid=(n // tm,),
        in_specs=[
            pl.BlockSpec((tm, d), row), mod_spec, _const_spec((1, d)),
            _const_spec(wdc.shape), _const_spec(gq.shape), _const_spec(gkv.shape),
            pl.BlockSpec((tm, LANES), row), pl.BlockSpec((tm, LANES), row),
            _const_spec(wqn.shape), _const_spec(wqp.shape), _const_spec(wqs.shape),
            _const_spec(wukt.shape),
        ],
        out_specs=[
            pl.BlockSpec((tm, KV_LORA), row), pl.BlockSpec((tm, QK_ROPE), row),
            pl.BlockSpec((tm, MLA_HEADS * KV_LORA), row),
            pl.BlockSpec((tm, MLA_HEADS * QK_ROPE), row),
        ],
        out_shape=[
            jax.ShapeDtypeStruct((n, KV_LORA), F32), jax.ShapeDtypeStruct((n, QK_ROPE), F32),
            jax.ShapeDtypeStruct((n, MLA_HEADS * KV_LORA), BF16),
            jax.ShapeDtypeStruct((n, MLA_HEADS * QK_ROPE), BF16),
        ],
        compiler_params=_cparams("parallel"),
        name="mla_proj_sample",
    )(x, mod, g1, wdc, gq, gkv, tc, ts, wqn, wqp, wqs, wukt)


def _attn_prompt_kernel(q_ref, k_ref, v_ref, o_ref, *, tq):
    qi = pl.program_id(2)
    q = q_ref[...]

    def block(kb, carry, masked):
        m, l, acc = carry
        k0 = pl.multiple_of(kb * tq, tq)
        k = k_ref[pl.ds(k0, tq), :]
        v = v_ref[pl.ds(k0, tq), :]
        s = lax.dot_general(q, k, (((1,), (1,)), ((), ())), preferred_element_type=F32)
        if masked:
            rq = lax.broadcasted_iota(jnp.int32, (tq, tq), 0) // CHUNK
            ck = lax.broadcasted_iota(jnp.int32, (tq, tq), 1) // CHUNK
            s = jnp.where(rq >= ck, s, NEG_BIG)
        m_new = jnp.maximum(m, jnp.max(s, axis=-1, keepdims=True))
        alpha = jnp.exp(m - m_new)
        p = jnp.exp(s - m_new)
        l = alpha * l + jnp.sum(p, axis=-1, keepdims=True)
        acc = alpha * acc + _dot(p.astype(BF16), v)
        return m_new, l, acc

    init = (jnp.full((tq, 1), NEG_BIG, F32), jnp.zeros((tq, 1), F32),
            jnp.zeros((tq, V_HD), F32))
    carry = lax.fori_loop(0, qi, lambda kb, c: block(kb, c, False), init)
    m, l, acc = block(qi, carry, True)
    o_ref[...] = (acc / l).astype(o_ref.dtype)


def _attn_prompt(q, k, v, *, tq):
    b, s, _ = q.shape
    return pl.pallas_call(
        functools.partial(_attn_prompt_kernel, tq=tq),
        grid=(b, MLA_HEADS, s // tq),
        in_specs=[
            pl.BlockSpec((None, tq, QK_CAT), lambda bi, hi, qi: (bi, qi, hi)),
            pl.BlockSpec((None, s, QK_CAT), lambda bi, hi, qi: (bi, 0, hi)),
            pl.BlockSpec((None, s, V_HD), lambda bi, hi, qi: (bi, 0, hi)),
        ],
        out_specs=pl.BlockSpec((None, tq, V_HD), lambda bi, hi, qi: (bi, qi, hi)),
        out_shape=jax.ShapeDtypeStruct((b, s, MLA_HEADS * V_HD), BF16),
        compiler_params=_cparams("parallel", "parallel", "arbitrary"),
        name="attn_prompt",
    )(q, k, v)


def _attn_sample_kernel(ql_ref, qp_ref, cc_ref, cp_ref, nc_ref, np_ref, o_ref):
    ql = ql_ref[...]
    qp = qp_ref[...]
    nt = (((1,), (1,)), ((), ()))
    cc = cc_ref[...].astype(BF16)
    cp = cp_ref[...].astype(BF16)
    nc = nc_ref[...].astype(BF16)
    npe = np_ref[...].astype(BF16)
    s_c = (lax.dot_general(ql, cc, nt, preferred_element_type=F32)
           + lax.dot_general(qp, cp, nt, preferred_element_type=F32))
    s_n = (lax.dot_general(ql, nc, nt, preferred_element_type=F32)
           + lax.dot_general(qp, npe, nt, preferred_element_type=F32))
    m = jnp.maximum(jnp.max(s_c, axis=-1, keepdims=True), jnp.max(s_n, axis=-1, keepdims=True))
    p_c = jnp.exp(s_c - m)
    p_n = jnp.exp(s_n - m)
    l = jnp.sum(p_c, axis=-1, keepdims=True) + jnp.sum(p_n, axis=-1, keepdims=True)
    o = _dot(p_c.astype(BF16), cc) + _dot(p_n.astype(BF16), nc)
    o_ref[...] = (o / l).astype(o_ref.dtype)


def _attn_sample(qlat, qpe, cache_ckv, cache_kpe, ckv_new, kpe_new):
    nb, past, _ = cache_ckv.shape
    r = qlat.shape[1]
    t = ckv_new.shape[1]
    blk = lambda shape: pl.BlockSpec((None,) + shape, lambda i: (i, 0, 0))
    return pl.pallas_call(
        _attn_sample_kernel,
        grid=(nb,),
        in_specs=[blk((r, KV_LORA)), blk((r, QK_ROPE)), blk((past, KV_LORA)),
                  blk((past, QK_ROPE)), blk((t, KV_LORA)), blk((t, QK_ROPE))],
        out_specs=blk((r, KV_LORA)),
        out_shape=jax.ShapeDtypeStruct((nb, r, KV_LORA), BF16),
        compiler_params=_cparams("parallel"),
        name="attn_sample",
    )(qlat, qpe, cache_ckv, cache_kpe, ckv_new, kpe_new)


def _oproj_kernel(x_ref, mod_ref, o_ref, wo_ref, xo_ref):
    xo_ref[...] = x_ref[...] + mod_ref[2] * _dot(o_ref[...], wo_ref[...])


def _oproj_latent_kernel(x_ref, mod_ref, o_ref, wuv_ref, wo_ref, xo_ref):
    acc = None
    for hh in range(MLA_HEADS):
        oh = _dot(o_ref[:, hh * KV_LORA:(hh + 1) * KV_LORA], wuv_ref[hh]).astype(BF16)
        part = _dot(oh, wo_ref[hh * V_HD:(hh + 1) * V_HD, :])
        acc = part if acc is None else acc + part
    xo_ref[...] = x_ref[...] + mod_ref[2] * acc


def _oproj(x, mod, mod_spec, o, wo, wuv=None, *, tm):
    n, d = x.shape
    row = lambda i: (i, 0)
    in_specs = [pl.BlockSpec((tm, d), row), mod_spec, pl.BlockSpec((tm, o.shape[1]), row)]
    args = [x, mod, o]
    if wuv is not None:
        in_specs.append(_const_spec(wuv.shape))
        args.append(wuv)
    in_specs.append(_const_spec(wo.shape))
    args.append(wo)
    return pl.pallas_call(
        _oproj_kernel if wuv is None else _oproj_latent_kernel,
        grid=(n // tm,),
        in_specs=in_specs,
        out_specs=pl.BlockSpec((tm, d), row),
        out_shape=jax.ShapeDtypeStruct((n, d), F32),
        compiler_params=_cparams("parallel"),
        name="oproj" if wuv is None else "oproj_latent",
    )(*args)


def _moe_kernel(x_ref, mod_ref, g_ref, wrh_ref, wrl_ref, br_ref, wg_ref, wu_ref, wd_ref,
                fg_ref, y_ref, h_ref, comb_ref, acc_ref):
    e = pl.program_id(1)
    tm = x_ref.shape[0]
    lane = lax.broadcasted_iota(jnp.int32, (tm, LANES), 1)

    @pl.when(e == 0)
    def _():
        hf = _modulate(x_ref[...], g_ref[...], mod_ref[0], mod_ref[1])
        hb = hf.astype(BF16)
        h_ref[...] = hb
        h_lo = (hf - hb.astype(F32)).astype(BF16)
        logits = (_dot(hb, wrh_ref[...]) + _dot(hb, wrl_ref[...])
                  + _dot(h_lo, wrh_ref[...])) + br_ref[...]
        m1 = jnp.max(logits, axis=-1, keepdims=True)
        i1 = jnp.min(jnp.where(logits == m1, lane, LANES), axis=-1, keepdims=True)
        rest = jnp.where(lane == i1, -jnp.inf, logits)
        m2 = jnp.max(rest, axis=-1, keepdims=True)
        i2 = jnp.min(jnp.where(rest == m2, lane, LANES), axis=-1, keepdims=True)
        e2 = jnp.exp(m2 - m1)
        den = 1.0 + e2
        comb_ref[...] = jnp.where(lane == i1, 1.0 / den, 0.0) + jnp.where(lane == i2, e2 / den, 0.0)
        acc_ref[...] = jnp.zeros(acc_ref.shape, F32)

    h = h_ref[...]
    w_e = jnp.sum(jnp.where(lane == e, comb_ref[...], 0.0), axis=-1, keepdims=True)
    act = (jax.nn.silu(_dot(h, wg_ref[...])) * _dot(h, wu_ref[...]) * w_e).astype(BF16)
    acc_ref[...] += _dot(act, wd_ref[...])

    @pl.when(e == pl.num_programs(1) - 1)
    def _():
        xn = x_ref[...] + mod_ref[2] * acc_ref[...]
        y_ref[...] = xn * lax.rsqrt(jnp.mean(xn * xn, axis=-1, keepdims=True) + EPS) * fg_ref[...]


def _moe(x, mod, mod_spec2, g2, wrh, wrl, br, wg, wu, wd, fg, *, tm):
    n, d = x.shape
    ne, _, ff = wg.shape
    return pl.pallas_call(
        _moe_kernel,
        grid=(n // tm, ne),
        in_specs=[
            pl.BlockSpec((tm, d), lambda i, e: (i, 0)),
            mod_spec2,
            _const_spec((1, d)),
            _const_spec(wrh.shape), _const_spec(wrl.shape), _const_spec(br.shape),
            pl.BlockSpec((None, d, ff), lambda i, e: (e, 0, 0)),
            pl.BlockSpec((None, d, ff), lambda i, e: (e, 0, 0)),
            pl.BlockSpec((None, ff, d), lambda i, e: (e, 0, 0)),
            _const_spec((1, d)),
        ],
        out_specs=pl.BlockSpec((tm, d), lambda i, e: (i, 0)),
        out_shape=jax.ShapeDtypeStruct((n, d), F32),
        scratch_shapes=[pltpu.VMEM((tm, d), BF16), pltpu.VMEM((tm, LANES), F32),
                        pltpu.VMEM((tm, d), F32)],
        compiler_params=_cparams("parallel", "arbitrary"),
        name="moe",
    )(x, mod, g2, wrh, wrl, br, wg, wu, wd, fg)


def _rope_tables(pos):
    half = QK_ROPE // 2
    inv = jnp.exp(-math.log(ROPE_BASE) * jnp.arange(half, dtype=F32) / half)
    ang = pos.astype(F32)[:, None] * inv[None, :]
    cos, sin = jnp.cos(ang), jnp.sin(ang)
    reps = LANES // QK_ROPE
    tc = jnp.tile(jnp.concatenate([cos, cos], axis=-1), (1, reps))
    ts = jnp.tile(jnp.concatenate([-sin, sin], axis=-1), (1, reps))
    return tc, ts


def _swap_halves(w):
    half = w.shape[-1] // 2
    return jnp.concatenate([w[..., half:], w[..., :half]], axis=-1)


def kernel(x_prompt, x_sample, c_prompt, c_sample, state_conv, cache_ckv, cache_kpe, norm1_g, norm2_g, w_ada, b_ada, w_in_ab, b_in_ab, conv_w, conv_b, ln_conv_g, ln_conv_b, ln_v_g, ln_v_b, w_spatial, b_spatial, w_out_ab, b_out_ab, w_ffn_gu, w_ffn_down, w_dc, g_q, g_kv, w_uq, w_uk, w_uv, w_o, w_router, b_router, w_exp_gu, w_exp_down, final_g):
    nb, seq, d = x_prompt.shape
    ns_b, t_s, _ = x_sample.shape
    past = cache_ckv.shape[2]
    c_ch = conv_w.shape[-1]
    n_p = nb * seq
    n_s = ns_b * t_s
    tm = 512
    tiles_per_seq = seq // tm
    s_group = GMLP_CHUNK // t_s

    xp = x_prompt.reshape(n_p, d)
    xs = x_sample.reshape(n_s, d)

    mods = _adaln(jnp.concatenate([c_prompt, c_sample], axis=0), w_ada, b_ada)
    depth = w_ada.shape[0]
    mods = mods.reshape(depth, nb + ns_b, 6, d)

    def prompt_mod(l, k0):
        return mods[l, :nb, k0:k0 + 3][:, :, None, :]

    def sample_mod(l, k0):
        m = mods[l, nb:, k0:k0 + 3]
        m = jnp.broadcast_to(m[:, None], (ns_b, t_s, 3, d)).reshape(n_s, 3, d)
        return m.transpose(1, 0, 2)

    def pspec(tiles_per_b):
        return pl.BlockSpec((None, 3, 1, d), lambda i: (i // tiles_per_b, 0, 0, 0))

    def sspec(rows):
        return pl.BlockSpec((3, rows, d), lambda i: (0, i, 0))

    g1 = norm1_g[0][None]
    g2 = norm2_g[0][None]
    w_in = w_in_ab[0].astype(BF16)
    woa = w_out_ab[0][:c_ch].astype(BF16)
    wob = w_out_ab[0][c_ch:].astype(BF16)
    hd = c_ch // GMLP_HEADS
    bsp_p = jnp.repeat(b_spatial[0].T, hd, axis=1)
    wsp_p = w_spatial[0]
    eye = jnp.eye(s_group, dtype=F32)
    wsp_s = jnp.einsum('ab,hts->hatbs', eye, w_spatial[0][:, :t_s, :t_s]).reshape(
        GMLP_HEADS, GMLP_CHUNK, GMLP_CHUNK)
    bsp_s = jnp.tile(jnp.repeat(b_spatial[0][:, :t_s].T, hd, axis=1), (s_group, 1))
    common = (w_in, b_in_ab[0][None], conv_w[0], conv_b[0][None], ln_conv_g[0][None],
              ln_conv_b[0][None], ln_v_g[0][None], ln_v_b[0][None])
    tail = (woa, wob, b_out_ab[0][None])

    pm = prompt_mod(0, 0)
    xp, conv_p = _mixer(
        xp, pm, pl.BlockSpec((None, 3, 1, d), lambda o, i: (o, 0, 0, 0)), g1,
        common + (wsp_p, bsp_p) + tail, ns=1, t=tm, n_outer=nb, n_inner=tiles_per_seq)
    sm = sample_mod(0, 0)
    n_st = n_s // GMLP_CHUNK
    xs, conv_s, gv_s = _mixer(
        xs, sm, pl.BlockSpec((3, GMLP_CHUNK, d), lambda o, i: (0, o * n_st + i, 0)), g1,
        common + (wsp_s, bsp_s) + tail, ns=s_group, t=t_s, n_outer=1, n_inner=n_st,
        hist=state_conv[0])

    ff = w_ffn_gu.shape[-1] // 2
    wg = w_ffn_gu[0][:, :ff].astype(BF16)
    wu = w_ffn_gu[0][:, ff:].astype(BF16)
    wd = w_ffn_down[0].astype(BF16)
    xp = _ffn(xp, prompt_mod(0, 3), pspec(tiles_per_seq), g2, wg, wu, wd, tm=tm)
    xs = _ffn(xs, sample_mod(0, 3), sspec(n_s), g2, wg, wu, wd, tm=n_s)

    g1 = norm1_g[1][None]
    g2 = norm2_g[1][None]
    scale = (QK_NOPE + QK_ROPE) ** -0.5
    zeros64 = jnp.zeros((d, QK_ROPE), F32)
    w_kpe = w_dc[0][:, Q_LORA + KV_LORA:]
    wdc = jnp.concatenate([w_dc[0][:, :Q_LORA + KV_LORA], w_kpe, zeros64,
                           _swap_halves(w_kpe), zeros64], axis=1).astype(BF16)
    wq3 = w_uq[0].reshape(Q_LORA, MLA_HEADS, QK_NOPE + QK_ROPE) * scale
    wq_nope = wq3[:, :, :QK_NOPE]
    wq_pe = wq3[:, :, QK_NOPE:]
    wq_sw = _swap_halves(wq_pe)
    zpad = jnp.zeros((Q_LORA, MLA_HEADS, QK_ROPE), F32)
    wqa = jnp.concatenate([wq_nope, wq_pe, zpad], axis=-1).reshape(Q_LORA, -1).astype(BF16)
    wqb = jnp.concatenate([wq_sw, zpad], axis=-1).reshape(Q_LORA, -1).astype(BF16)
    wuk = w_uk[0].astype(BF16)
    wuv = w_uv[0].astype(BF16)
    gq = g_q[0][None]
    gkv = g_kv[0][None]
    wo = w_o[0].astype(BF16)

    tc_p, ts_p = _rope_tables(jnp.arange(seq))
    pm = prompt_mod(1, 0)
    ckv_p, kpe_p, q_p, k_p, v_p = _mla_prompt(
        xp, pm, pspec(tiles_per_seq), g1, (wdc, gq, gkv, wqa, wqb, wuk, wuv),
        tc_p, ts_p, tm=tm, tiles_per_seq=tiles_per_seq)
    o_p = _attn_prompt(q_p.reshape(nb, seq, -1), k_p.reshape(nb, seq, -1),
                       v_p.reshape(nb, seq, -1), tq=512)
    xp = _oproj(xp, pm, pspec(tiles_per_seq), o_p.reshape(n_p, -1), wo, tm=tm)

    tc_s, ts_s = _rope_tables(past + jnp.arange(t_s))
    tc_s = jnp.tile(tc_s, (ns_b, 1))
    ts_s = jnp.tile(ts_s, (ns_b, 1))
    wqn = wq_nope.reshape(Q_LORA, -1).astype(BF16)
    wqp = wq_pe.reshape(Q_LORA, -1).astype(BF16)
    wqs = wq_sw.reshape(Q_LORA, -1).astype(BF16)
    wukt = w_uk[0].reshape(KV_LORA, MLA_HEADS, QK_NOPE).transpose(1, 2, 0).astype(BF16)
    wuv_h = w_uv[0].reshape(KV_LORA, MLA_HEADS, V_HD).transpose(1, 0, 2).astype(BF16)
    sm = sample_mod(1, 0)
    ckv_s, kpe_s, qlat, qpe = _mla_sample(
        xs, sm, sspec(n_s), g1, (wdc, gq, gkv, wqn, wqp, wqs, wukt), tc_s, ts_s, tm=n_s)
    o_lat = _attn_sample(
        qlat.reshape(ns_b, t_s * MLA_HEADS, KV_LORA), qpe.reshape(ns_b, t_s * MLA_HEADS, QK_ROPE),
        cache_ckv[0], cache_kpe[0], ckv_s.reshape(ns_b, t_s, KV_LORA),
        kpe_s.reshape(ns_b, t_s, QK_ROPE))
    xs = _oproj(xs, sm, sspec(n_s), o_lat.reshape(n_s, MLA_HEADS * KV_LORA), wo, wuv_h, tm=n_s)

    ne = w_router.shape[-1]
    eff = w_exp_gu.shape[-1] // 2
    wr = jnp.zeros((d, LANES), F32).at[:, :ne].set(w_router[0])
    wrh = wr.astype(BF16)
    wrl = (wr - wrh.astype(F32)).astype(BF16)
    br = jnp.full((1, LANES), NEG_BIG, F32).at[0, :ne].set(b_router[0])
    weg = w_exp_gu[0][:, :, :eff].astype(BF16)
    weu = w_exp_gu[0][:, :, eff:].astype(BF16)
    wed = w_exp_down[0].astype(BF16)
    fg = final_g[None]
    yp = _moe(xp, prompt_mod(1, 3), pl.BlockSpec((None, 3, 1, d), lambda i, e: (i // tiles_per_seq, 0, 0, 0)),
              g2, wrh, wrl, br, weg, weu, wed, fg, tm=tm)
    ys = _moe(xs, sample_mod(1, 3), pl.BlockSpec((3, n_s, d), lambda i, e: (0, i, 0)),
              g2, wrh, wrl, br, weg, weu, wed, fg, tm=n_s)

    return (yp.reshape(nb, seq, d), ys.reshape(ns_b, t_s, d),
            conv_p[None], conv_s[None], gv_s.reshape(ns_b, t_s, c_ch)[None],
            ckv_p.reshape(nb, seq, KV_LORA)[None], kpe_p.reshape(nb, seq, QK_ROPE)[None],
            ckv_s.reshape(ns_b, t_s, KV_LORA)[None], kpe_s.reshape(ns_b, t_s, QK_ROPE)[None])
```

```python
import functools
import math

import jax
import jax.numpy as jnp
from jax import lax
from jax.experimental import pallas as pl
from jax.experimental.pallas import tpu as pltpu

F32 = jnp.float32
BF16 = jnp.bfloat16

EPS = 1e-6
CHUNK = 64
CONV_W = 31
HIST = CONV_W - 1
HIST_PAD = 32
GMLP_CHUNK = 128
GMLP_HEADS = 8
MLA_HEADS = 8
Q_LORA = 256
KV_LORA = 256
QK_NOPE = 128
QK_ROPE = 64
V_HD = 128
QK_CAT = 256
ROPE_BASE = 10000.0
N_EXPERTS = 8
MOE_ROWS = 256
LANES = 128
NEG_BIG = -1e30

VMEM_LIMIT = 56 * 1024 * 1024


def _cparams(*sem):
    return pltpu.CompilerParams(dimension_semantics=sem, vmem_limit_bytes=VMEM_LIMIT)


def _const_spec(shape):
    nd = len(shape)
    return pl.BlockSpec(shape, lambda *_: (0,) * nd, pipeline_mode=pl.Buffered(1))


def _modulate(x, g, shift, scale):
    y = x * lax.rsqrt(jnp.mean(x * x, axis=-1, keepdims=True) + EPS)
    return (y * g) * (1.0 + scale) + shift


def _layer_norm(x, g, b):
    mu = jnp.mean(x, axis=-1, keepdims=True)
    xc = x - mu
    var = jnp.mean(xc * xc, axis=-1, keepdims=True)
    return xc * lax.rsqrt(var + EPS) * g + b


def _dot(a, b):
    return jnp.dot(a, b, preferred_element_type=F32)


_NT = (((1,), (1,)), ((), ()))


def _adaln_kernel(c_ref, w_ref, b_ref, o_ref):
    c = c_ref[...]
    a = jax.nn.silu(c).astype(BF16)
    o_ref[...] = _dot(a, w_ref[...].astype(BF16)) + b_ref[...]


def _adaln(c_all, w_ada, b_ada, tn=1536):
    depth, d, n = w_ada.shape
    rows = c_all.shape[0]
    return pl.pallas_call(
        _adaln_kernel,
        grid=(depth, n // tn),
        in_specs=[
            pl.BlockSpec((rows, d), lambda l, j: (0, 0)),
            pl.BlockSpec((None, d, tn), lambda l, j: (l, 0, j)),
            pl.BlockSpec((None, 1, tn), lambda l, j: (l, 0, j)),
        ],
        out_specs=pl.BlockSpec((None, rows, tn), lambda l, j: (l, 0, j)),
        out_shape=jax.ShapeDtypeStruct((depth, rows, n), F32),
        compiler_params=_cparams("parallel", "parallel"),
        name="adaln",
    )(c_all, w_ada, b_ada.reshape(depth, 1, n))


def _mixer_kernel(*refs, ns, t, carry_hist, conv_rows):
    if carry_hist:
        (x_ref, mod_ref, g_ref, win_ref, bin_ref, cw_ref, cb_ref, lag_ref, lab_ref,
         lvg_ref, lvb_ref, wsp_ref, bsp_ref, woa_ref, wob_ref, bout_ref,
         xo_ref, conv_ref, aext_ref, ya_ref, yb_ref, tapwin_ref) = refs
        hist_ref = v_ref = None
    else:
        (x_ref, mod_ref, g_ref, win_ref, bin_ref, cw_ref, cb_ref, lag_ref, lab_ref,
         lvg_ref, lvb_ref, wsp_ref, bsp_ref, woa_ref, wob_ref, bout_ref, hist_ref,
         xo_ref, conv_ref, v_ref, aext_ref, ya_ref, yb_ref, tapwin_ref) = refs
    c = cw_ref.shape[1]
    rows = ns * t

    x = x_ref[...]
    h = _modulate(x, g_ref[...], mod_ref[0], mod_ref[1]).astype(BF16)
    p = _dot(h, win_ref[...]) + bin_ref[...]

    a = p[:, :c] * jax.nn.sigmoid(p[:, c:2 * c])
    if carry_hist:
        @pl.when(pl.program_id(1) == 0)
        def _():
            aext_ref[:, :HIST_PAD, :] = jnp.zeros((ns, HIST_PAD, c), F32)
    else:
        aext_ref[:, HIST_PAD - HIST:HIST_PAD, :] = hist_ref[...]
    aext_ref[:, HIST_PAD:, :] = a.reshape(ns, t, c)

    off = HIST_PAD - HIST
    n_cc = t // conv_rows
    lag = lag_ref[...]
    lab = lab_ref[...]
    cb = cb_ref[...]

    def conv_block(src_ref, s):
        acc = jnp.zeros((conv_rows, c), F32)
        for k in range(CONV_W):
            acc = acc + src_ref[s, off + k:off + k + conv_rows, :] * cw_ref[k:k + 1, :]
        y = _layer_norm(acc + cb, lag, lab)
        return jax.nn.silu(y).astype(BF16)

    for s in range(ns):
        if n_cc == 1:
            ya_ref[s * t:(s + 1) * t, :] = conv_block(aext_ref, s)
        else:
            def body(i, carry, s=s):
                r0 = pl.multiple_of(i * conv_rows, conv_rows)
                tapwin_ref[0] = aext_ref[s, pl.ds(r0, conv_rows + HIST_PAD), :]
                ya_ref[pl.ds(s * t + r0, conv_rows), :] = conv_block(tapwin_ref, 0)
                return carry
            lax.fori_loop(0, n_cc, body, 0)

    conv_ref[...] = aext_ref[:, t + HIST_PAD - HIST:t + HIST_PAD, :]
    if carry_hist:
        aext_ref[:, :HIST_PAD, :] = aext_ref[:, t:t + HIST_PAD, :]

    z = jax.nn.gelu(p[:, 2 * c:])
    u = z[:, :c]
    v = _layer_norm(z[:, c:], lvg_ref[...], lvb_ref[...])
    if v_ref is not None:
        v_ref[...] = v
    ri = lax.broadcasted_iota(jnp.int32, (GMLP_CHUNK, GMLP_CHUNK), 0)
    ci = lax.broadcasted_iota(jnp.int32, (GMLP_CHUNK, GMLP_CHUNK), 1)
    tril = ri >= ci
    low_half = ci < (LANES // 2)
    w_heads = [jnp.where(tril, wsp_ref[hh], 0.0).astype(BF16) for hh in range(GMLP_HEADS)]
    for blk in range(rows // GMLP_CHUNK):
        r = slice(blk * GMLP_CHUNK, (blk + 1) * GMLP_CHUNK)
        for j in range(c // LANES):
            l = slice(j * LANES, (j + 1) * LANES)
            vp = v[r, l]
            lo = jnp.where(low_half, vp, 0.0).astype(BF16)
            hi = jnp.where(low_half, 0.0, vp).astype(BF16)
            sp = _dot(w_heads[2 * j], lo) + _dot(w_heads[2 * j + 1], hi)
            yb_ref[r, l] = (u[r, l] * (sp + bsp_ref[:, l])).astype(BF16)

    out = _dot(ya_ref[...], woa_ref[...]) + _dot(yb_ref[...], wob_ref[...]) + bout_ref[...]
    xo_ref[...] = x + mod_ref[2] * out


def _mixer(x, mod, mod_spec, g1, wts, *, ns, t, n_outer, n_inner, hist=None):
    (w_in, b_in, cw, cb, lag, lab, lvg, lvb, wsp, bsp, woa, wob, b_out) = wts
    d = x.shape[1]
    c = cw.shape[1]
    rows = ns * t
    carry_hist = hist is None
    tile = lambda o, i: (o * n_inner + i, 0)
    in_specs = [
        pl.BlockSpec((rows, d), tile),
        mod_spec,
        _const_spec((1, d)),
        _const_spec(w_in.shape), _const_spec(b_in.shape),
        _const_spec(cw.shape), _const_spec(cb.shape),
        _const_spec(lag.shape), _const_spec(lab.shape),
        _const_spec(lvg.shape), _const_spec(lvb.shape),
        _const_spec(wsp.shape), _const_spec(bsp.shape),
        _const_spec(woa.shape), _const_spec(wob.shape), _const_spec(b_out.shape),
    ]
    args = [x, mod, g1, w_in, b_in, cw, cb, lag, lab, lvg, lvb, wsp, bsp, woa, wob, b_out]
    n_tiles = n_outer * n_inner
    out_shape = [jax.ShapeDtypeStruct(x.shape, F32)]
    out_specs = [pl.BlockSpec((rows, d), tile)]
    if carry_hist:
        out_shape.append(jax.ShapeDtypeStruct((n_outer * ns, HIST, c), F32))
        out_specs.append(pl.BlockSpec((ns, HIST, c), lambda o, i: (o, 0, 0)))
    else:
        in_specs.append(pl.BlockSpec((ns, HIST, c), lambda o, i: (o * n_inner + i, 0, 0)))
        args.append(hist)
        out_shape.append(jax.ShapeDtypeStruct((n_tiles * ns, HIST, c), F32))
        out_specs.append(pl.BlockSpec((ns, HIST, c), lambda o, i: (o * n_inner + i, 0, 0)))
        out_shape.append(jax.ShapeDtypeStruct((x.shape[0], c), F32))
        out_specs.append(pl.BlockSpec((rows, c), tile))
    conv_rows = min(t, 64)
    return pl.pallas_call(
        functools.partial(_mixer_kernel, ns=ns, t=t, carry_hist=carry_hist, conv_rows=conv_rows),
        grid=(n_outer, n_inner),
        in_specs=in_specs,
        out_specs=out_specs,
        out_shape=out_shape,
        scratch_shapes=[
            pltpu.VMEM((ns, t + HIST_PAD, c), F32),
            pltpu.VMEM((rows, c), BF16),
            pltpu.VMEM((rows, c), BF16),
            pltpu.VMEM((1, conv_rows + HIST_PAD, c), F32),
        ],
        compiler_params=_cparams("arbitrary", "arbitrary"),
        name="mixer_prompt" if carry_hist else "mixer_sample",
    )(*args)


def _ffn_kernel(x_ref, mod_ref, g_ref, wg_ref, wu_ref, wd_ref, o_ref, *, n_chunks):
    x = x_ref[...]
    h = _modulate(x, g_ref[...], mod_ref[0], mod_ref[1]).astype(BF16)
    ff = wg_ref.shape[1]
    cw = ff // n_chunks
    acc = None
    for j in range(n_chunks):
        sl = slice(j * cw, (j + 1) * cw)
        gate = _dot(h, wg_ref[:, sl])
        up = _dot(h, wu_ref[:, sl])
        act = (jax.nn.silu(gate) * up).astype(BF16)
        part = _dot(act, wd_ref[sl, :])
        acc = part if acc is None else acc + part
    o_ref[...] = x + mod_ref[2] * acc


def _ffn(x, mod, mod_spec, g2, wg, wu, wd, *, tm):
    n, d = x.shape
    return pl.pallas_call(
        functools.partial(_ffn_kernel, n_chunks=2),
        grid=(n // tm,),
        in_specs=[
            pl.BlockSpec((tm, d), lambda i: (i, 0)),
            mod_spec,
            _const_spec((1, d)),
            _const_spec(wg.shape), _const_spec(wu.shape), _const_spec(wd.shape),
        ],
        out_specs=pl.BlockSpec((tm, d), lambda i: (i, 0)),
        out_shape=jax.ShapeDtypeStruct((n, d), F32),
        compiler_params=_cparams("parallel"),
        name="ffn",
    )(x, mod, g2, wg, wu, wd)


def _mla_latents(x_ref, mod_ref, g_ref, wdc_ref, gq_ref, gkv_ref, tc, ts):
    x = x_ref[...]
    h = _modulate(x, g_ref[...], mod_ref[0], mod_ref[1]).astype(BF16)
    p = _dot(h, wdc_ref[...])
    cq = p[:, :Q_LORA]
    cq = cq * lax.rsqrt(jnp.mean(cq * cq, axis=-1, keepdims=True) + EPS) * gq_ref[...]
    ckv = p[:, Q_LORA:Q_LORA + KV_LORA]
    ckv = ckv * lax.rsqrt(jnp.mean(ckv * ckv, axis=-1, keepdims=True) + EPS) * gkv_ref[...]
    b0 = Q_LORA + KV_LORA
    kpe = p[:, b0:b0 + LANES] * tc + p[:, b0 + LANES:b0 + 2 * LANES] * ts
    return cq.astype(BF16), ckv, kpe


def _mla_prompt_kernel(x_ref, mod_ref, g_ref, wdc_ref, gq_ref, gkv_ref, tc_ref, ts_ref,
                       wqa_ref, wqb_ref, wuk_ref, wuvt_ref,
                       ckv_ref, kpe_ref, q_ref, k_ref, vt_ref):
    tc = tc_ref[...]
    ts = ts_ref[...]
    cq, ckv, kpe = _mla_latents(x_ref, mod_ref, g_ref, wdc_ref, gq_ref, gkv_ref, tc, ts)
    ckv_ref[...] = ckv
    kpe_ref[...] = kpe[:, :QK_ROPE]
    qa = _dot(cq, wqa_ref[...])
    qb = _dot(cq, wqb_ref[...])
    ckv_b = ckv.astype(BF16)
    kn = _dot(ckv_b, wuk_ref[...])
    vt_ref[...] = lax.dot_general(wuvt_ref[...], ckv_b, _NT, preferred_element_type=F32).astype(BF16)
    kpe_b = kpe.astype(BF16)
    for hh in range(MLA_HEADS):
        base = hh * QK_CAT
        q_ref[:, base:base + QK_NOPE] = qa[:, base:base + QK_NOPE].astype(BF16)
        q_ref[:, base + QK_NOPE:base + QK_CAT] = (
            qa[:, base + QK_NOPE:base + QK_CAT] * tc
            + qb[:, hh * LANES:(hh + 1) * LANES] * ts).astype(BF16)
        k_ref[:, base:base + QK_NOPE] = kn[:, hh * QK_NOPE:(hh + 1) * QK_NOPE].astype(BF16)
        k_ref[:, base + QK_NOPE:base + QK_CAT] = kpe_b


def _mla_prompt(x, mod, mod_spec, g1, wts, tc, ts, *, tm, tiles_per_seq):
    wdc, gq, gkv, wqa, wqb, wuk, wuvt = wts
    n, d = x.shape
    hq = MLA_HEADS * QK_CAT
    hv = MLA_HEADS * V_HD
    nb = n // (tm * tiles_per_seq)
    row = lambda i: (i, 0)
    pos = lambda i: (i % tiles_per_seq, 0)
    return pl.pallas_call(
        _mla_prompt_kernel,
        grid=(n // tm,),
        in_specs=[
            pl.BlockSpec((tm, d), row), mod_spec, _const_spec((1, d)),
            _const_spec(wdc.shape), _const_spec(gq.shape), _const_spec(gkv.shape),
            pl.BlockSpec((tm, LANES), pos), pl.BlockSpec((tm, LANES), pos),
            _const_spec(wqa.shape), _const_spec(wqb.shape),
            _const_spec(wuk.shape), _const_spec(wuvt.shape),
        ],
        out_specs=[
            pl.BlockSpec((tm, KV_LORA), row), pl.BlockSpec((tm, QK_ROPE), row),
            pl.BlockSpec((tm, hq), row), pl.BlockSpec((tm, hq), row),
            pl.BlockSpec((None, hv, tm), lambda i: (i // tiles_per_seq, 0, i % tiles_per_seq)),
        ],
        out_shape=[
            jax.ShapeDtypeStruct((n, KV_LORA), F32), jax.ShapeDtypeStruct((n, QK_ROPE), F32),
            jax.ShapeDtypeStruct((n, hq), BF16), jax.ShapeDtypeStruct((n, hq), BF16),
            jax.ShapeDtypeStruct((nb, hv, tm * tiles_per_seq), BF16),
        ],
        compiler_params=_cparams("parallel"),
        name="mla_proj_prompt",
    )(x, mod, g1, wdc, gq, gkv, tc, ts, wqa, wqb, wuk, wuvt)


def _mla_sample_kernel(x_ref, mod_ref, g_ref, wdc_ref, gq_ref, gkv_ref, tc_ref, ts_ref,
                       wqn_ref, wqp_ref, wqs_ref, wukt_ref,
                       ckv_ref, kpe_ref, qlat_ref, qpe_ref):
    tc = tc_ref[...]
    ts = ts_ref[...]
    cq, ckv, kpe = _mla_latents(x_ref, mod_ref, g_ref, wdc_ref, gq_ref, gkv_ref, tc, ts)
    ckv_ref[...] = ckv
    kpe_ref[...] = kpe[:, :QK_ROPE]
    qn = _dot(cq, wqn_ref[...]).astype(BF16)
    qp = _dot(cq, wqp_ref[...])
    qs = _dot(cq, wqs_ref[...])
    for j in range(MLA_HEADS * QK_ROPE // LANES):
        l = slice(j * LANES, (j + 1) * LANES)
        qpe_ref[:, l] = (qp[:, l] * tc + qs[:, l] * ts).astype(BF16)
    for hh in range(MLA_HEADS):
        qlat_ref[:, hh * KV_LORA:(hh + 1) * KV_LORA] = _dot(
            qn[:, hh * QK_NOPE:(hh + 1) * QK_NOPE], wukt_ref[hh]).astype(BF16)


def _mla_sample(x, mod, mod_spec, g1, wts, tc, ts, *, tm):
    wdc, gq, gkv, wqn, wqp, wqs, wukt = wts
    n, d = x.shape
    row = lambda i: (i, 0)
    return pl.pallas_call(
        _mla_sample_kernel,
        grid=(n // tm,),
        in_specs=[
            pl.BlockSpec((tm, d), row), mod_spec, _const_spec((1, d)),
            _const_spec(wdc.shape), _const_spec(gq.shape), _const_spec(gkv.shape),
            pl.BlockSpec((tm, LANES), row), pl.BlockSpec((tm, LANES), row),
            _const_spec(wqn.shape), _const_spec(wqp.shape), _const_spec(wqs.shape),
            _const_spec(wukt.shape),
        ],
        out_specs=[
            pl.BlockSpec((tm, KV_LORA), row), pl.BlockSpec((tm, QK_ROPE), row),
            pl.BlockSpec((tm, MLA_HEADS * KV_LORA), row),
            pl.BlockSpec((tm, MLA_HEADS * QK_ROPE), row),
        ],
        out_shape=[
            jax.ShapeDtypeStruct((n, KV_LORA), F32), jax.ShapeDtypeStruct((n, QK_ROPE), F32),
            jax.ShapeDtypeStruct((n, MLA_HEADS * KV_LORA), BF16),
            jax.ShapeDtypeStruct((n, MLA_HEADS * QK_ROPE), BF16),
        ],
        compiler_params=_cparams("parallel"),
        name="mla_proj_sample",
    )(x, mod, g1, wdc, gq, gkv, tc, ts, wqn, wqp, wqs, wukt)


def _attn_prompt_kernel(q_ref, k_ref, vt_ref, o_ref, s_scr, p_scr, acc_scr, *, tq, tk):
    qi = pl.program_id(2)
    n_full = 2 * qi

    def qk_to(slot, t):
        k = k_ref[pl.ds(pl.multiple_of(t * tk, tk), tk), :]
        s = lax.dot_general(k, q_ref[...], _NT, preferred_element_type=F32)
        s_scr[slot] = s
        return jnp.max(s, axis=0, keepdims=True)

    def softmax_from(slot, m, l, bm, diag=None):
        if diag is not None:
            kk = (lax.broadcasted_iota(jnp.int32, (tk, tq), 0) + diag * tk) // CHUNK
            qq = lax.broadcasted_iota(jnp.int32, (tk, tq), 1) // CHUNK
            vis = qq >= kk
            bm = jnp.max(jnp.where(vis, s_scr[slot], NEG_BIG), axis=0, keepdims=True)
        m_new = jnp.maximum(m, bm)
        alpha = jnp.exp2(m - m_new)
        p = jnp.exp2(s_scr[slot] - m_new)
        if diag is not None:
            p = jnp.where(vis, p, 0.0)
        p_scr[slot] = p.astype(BF16)
        l = alpha * l + jnp.sum(p, axis=0, keepdims=True)
        return m_new, l, alpha

    def pv_from(slot, t, alpha):
        vt = vt_ref[:, pl.ds(pl.multiple_of(t * tk, tk), tk)]
        acc_scr[...] = alpha * acc_scr[...] + _dot(vt, p_scr[slot])

    p_scr[1] = jnp.zeros((tk, tq), BF16)
    acc_scr[...] = jnp.zeros((V_HD, tq), F32)
    m0 = jnp.full((1, tq), NEG_BIG, F32)
    l0 = jnp.zeros((1, tq), F32)
    bm0 = qk_to(0, 0)

    def body(i, carry):
        m, l, bm, a_prev = carry
        t = 2 * i
        bm1 = qk_to(1, t + 1)
        m, l, a0 = softmax_from(0, m, l, bm)
        pv_from(1, jnp.maximum(t - 1, 0), a_prev)
        bm2 = qk_to(0, t + 2)
        m, l, a1 = softmax_from(1, m, l, bm1)
        pv_from(0, t, a0)
        return m, l, bm2, a1

    m, l, _, a_prev = lax.fori_loop(0, qi, body, (m0, l0, bm0, jnp.ones((1, tq), F32)))
    pv_from(1, jnp.maximum(n_full - 1, 0), a_prev)
    qk_to(1, n_full + 1)
    m, l, a0 = softmax_from(0, m, l, None, diag=0)
    pv_from(0, n_full, a0)
    m, l, a1 = softmax_from(1, m, l, None, diag=1)
    pv_from(1, n_full + 1, a1)
    o_ref[...] = (acc_scr[...] / l).T.astype(o_ref.dtype)


def _attn_prompt(q, k, vt, *, tq):
    b, s, _ = q.shape
    tk = tq // 2
    return pl.pallas_call(
        functools.partial(_attn_prompt_kernel, tq=tq, tk=tk),
        grid=(b, MLA_HEADS, s // tq),
        in_specs=[
            pl.BlockSpec((None, tq, QK_CAT), lambda bi, hi, qi: (bi, qi, hi)),
            pl.BlockSpec((None, s, QK_CAT), lambda bi, hi, qi: (bi, 0, hi)),
            pl.BlockSpec((None, V_HD, s), lambda bi, hi, qi: (bi, hi, 0)),
        ],
        out_specs=pl.BlockSpec((None, tq, V_HD), lambda bi, hi, qi: (bi, qi, hi)),
        out_shape=jax.ShapeDtypeStruct((b, s, MLA_HEADS * V_HD), BF16),
        scratch_shapes=[pltpu.VMEM((2, tk, tq), F32), pltpu.VMEM((2, tk, tq), BF16),
                        pltpu.VMEM((V_HD, tq), F32)],
        compiler_params=_cparams("parallel", "parallel", "arbitrary"),
        name="attn_prompt",
    )(q, k, vt)


def _attn_sample_kernel(ql_ref, qp_ref, cc_ref, cp_ref, nc_ref, np_ref, o_ref):
    ql = ql_ref[...]
    qp = qp_ref[...]
    nt = (((1,), (1,)), ((), ()))
    cc = cc_ref[...].astype(BF16)
    cp = cp_ref[...].astype(BF16)
    nc = nc_ref[...].astype(BF16)
    npe = np_ref[...].astype(BF16)
    s_c = (lax.dot_general(ql, cc, nt, preferred_element_type=F32)
           + lax.dot_general(qp, cp, nt, preferred_element_type=F32))
    s_n = (lax.dot_general(ql, nc, nt, preferred_element_type=F32)
           + lax.dot_general(qp, npe, nt, preferred_element_type=F32))
    m = jnp.maximum(jnp.max(s_c, axis=-1, keepdims=True), jnp.max(s_n, axis=-1, keepdims=True))
    p_c = jnp.exp2(s_c - m)
    p_n = jnp.exp2(s_n - m)
    l = jnp.sum(p_c, axis=-1, keepdims=True) + jnp.sum(p_n, axis=-1, keepdims=True)
    o = _dot(p_c.astype(BF16), cc) + _dot(p_n.astype(BF16), nc)
    o_ref[...] = (o / l).astype(o_ref.dtype)


def _attn_sample(qlat, qpe, cache_ckv, cache_kpe, ckv_new, kpe_new):
    nb, past, _ = cache_ckv.shape
    r = qlat.shape[1]
    t = ckv_new.shape[1]
    blk = lambda shape: pl.BlockSpec((None,) + shape, lambda i: (i, 0, 0))
    return pl.pallas_call(
        _attn_sample_kernel,
        grid=(nb,),
        in_specs=[blk((r, KV_LORA)), blk((r, QK_ROPE)), blk((past, KV_LORA)),
                  blk((past, QK_ROPE)), blk((t, KV_LORA)), blk((t, QK_ROPE))],
        out_specs=blk((r, KV_LORA)),
        out_shape=jax.ShapeDtypeStruct((nb, r, KV_LORA), BF16),
        compiler_params=_cparams("parallel"),
        name="attn_sample",
    )(qlat, qpe, cache_ckv, cache_kpe, ckv_new, kpe_new)


def _route_tail(xm, mod_ref, g_ref, wrh_ref, wrl_ref, br_ref, xo_ref, h_ref, route_ref, cnt_ref):
    tm = xm.shape[0]
    xo_ref[...] = xm
    hf = _modulate(xm, g_ref[...], mod_ref[3], mod_ref[4])
    hb = hf.astype(BF16)
    h_ref[...] = hb
    h_lo = (hf - hb.astype(F32)).astype(BF16)
    logits = (_dot(hb, wrh_ref[...]) + _dot(hb, wrl_ref[...])
              + _dot(h_lo, wrh_ref[...])) + br_ref[...]
    lane = lax.broadcasted_iota(jnp.int32, (tm, LANES), 1)
    m1 = jnp.max(logits, axis=-1, keepdims=True)
    i1 = jnp.min(jnp.where(logits == m1, lane, LANES), axis=-1, keepdims=True)
    rest = jnp.where(lane == i1, -jnp.inf, logits)
    m2 = jnp.max(rest, axis=-1, keepdims=True)
    i2 = jnp.min(jnp.where(rest == m2, lane, LANES), axis=-1, keepdims=True)
    e2 = jnp.exp(m2 - m1)
    den = 1.0 + e2
    sel1 = lane == i1
    sel2 = lane == i2
    onehot = jnp.where(sel1, 1.0, jnp.where(sel2, 1.0, 0.0))
    ri = lax.broadcasted_iota(jnp.int32, (tm, tm), 0)
    ci = lax.broadcasted_iota(jnp.int32, (tm, tm), 1)
    before = jnp.where(ri > ci, 1.0, 0.0).astype(BF16)
    rank = _dot(before, onehot.astype(BF16))
    cnt = jnp.sum(onehot, axis=0, keepdims=True)
    nblk = jnp.floor((cnt + (MOE_ROWS - 1)) * (1.0 / MOE_ROWS))
    r8 = lax.broadcasted_iota(jnp.int32, (LANES, LANES), 0)
    c8 = lax.broadcasted_iota(jnp.int32, (LANES, LANES), 1)
    upper = jnp.where(r8 < c8, 1.0, 0.0).astype(BF16)
    first_blk = _dot(jnp.broadcast_to(nblk, (8, LANES)).astype(BF16), upper)[0:1]
    slot = first_blk * MOE_ROWS + rank
    d1 = jnp.sum(jnp.where(sel1, slot, 0.0), axis=-1, keepdims=True)
    d2 = jnp.sum(jnp.where(sel2, slot, 0.0), axis=-1, keepdims=True)
    route_ref[...] = jnp.where(lane == 0, d1, jnp.where(lane == 1, d2, jnp.where(
        lane == 2, 1.0 / den, jnp.where(lane == 3, e2 / den, 0.0))))
    cnt_ref[...] = jnp.broadcast_to(cnt, (8, LANES))


def _oproj_route_kernel(x_ref, mod_ref, o_ref, wo_ref, g_ref, wrh_ref, wrl_ref, br_ref,
                        xo_ref, h_ref, route_ref, cnt_ref):
    xm = x_ref[...] + mod_ref[2] * _dot(o_ref[...], wo_ref[...])
    _route_tail(xm, mod_ref, g_ref, wrh_ref, wrl_ref, br_ref, xo_ref, h_ref, route_ref, cnt_ref)


def _oproj_latent_route_kernel(x_ref, mod_ref, o_ref, wuv_ref, wo_ref, g_ref, wrh_ref, wrl_ref,
                               br_ref, xo_ref, h_ref, route_ref, cnt_ref):
    acc = None
    for hh in range(MLA_HEADS):
        oh = _dot(o_ref[:, hh * KV_LORA:(hh + 1) * KV_LORA], wuv_ref[hh]).astype(BF16)
        part = _dot(oh, wo_ref[hh * V_HD:(hh + 1) * V_HD, :])
        acc = part if acc is None else acc + part
    xm = x_ref[...] + mod_ref[2] * acc
    _route_tail(xm, mod_ref, g_ref, wrh_ref, wrl_ref, br_ref, xo_ref, h_ref, route_ref, cnt_ref)


def _oproj_route(x, mod, mod_spec, o, wo, g2, wrh, wrl, br, wuv=None, *, tm):
    n, d = x.shape
    row = lambda i: (i, 0)
    in_specs = [pl.BlockSpec((tm, d), row), mod_spec, pl.BlockSpec((tm, o.shape[1]), row)]
    args = [x, mod, o]
    if wuv is not None:
        in_specs.append(_const_spec(wuv.shape))
        args.append(wuv)
    in_specs += [_const_spec(wo.shape), _const_spec((1, d)), _const_spec(wrh.shape),
                 _const_spec(wrl.shape), _const_spec(br.shape)]
    args += [wo, g2, wrh, wrl, br]
    return pl.pallas_call(
        _oproj_route_kernel if wuv is None else _oproj_latent_route_kernel,
        grid=(n // tm,),
        in_specs=in_specs,
        out_specs=[pl.BlockSpec((tm, d), row), pl.BlockSpec((tm, d), row),
                   pl.BlockSpec((tm, LANES), row), pl.BlockSpec((None, 8, LANES), lambda i: (i, 0, 0))],
        out_shape=[jax.ShapeDtypeStruct((n, d), F32), jax.ShapeDtypeStruct((n, d), BF16),
                   jax.ShapeDtypeStruct((n, LANES), F32),
                   jax.ShapeDtypeStruct((n // tm, 8, LANES), F32)],
        compiler_params=_cparams("parallel"),
        name="oproj_route" if wuv is None else "oproj_latent_route",
    )(*args)


def _moe_routed_kernel(be_ref, nb_ref, h_ref, x_ref, mod_ref, rc_ref, rr_ref, wg_ref, wu_ref,
                       wd_ref, fg_ref, y_ref, acc_ref):
    i = pl.program_id(0)
    j = pl.program_id(1)
    tm = h_ref.shape[0]

    @pl.when(j == 0)
    def _():
        acc_ref[...] = jnp.zeros(acc_ref.shape, F32)

    @pl.when(j < nb_ref[i])
    def _():
        base = (j * MOE_ROWS).astype(F32)
        rr = rr_ref[...]
        srow = lax.broadcasted_iota(jnp.int32, (MOE_ROWS, tm), 0).astype(F32) + base
        hit1 = rr[0:1, :] == srow
        hit2 = rr[1:2, :] == srow
        sel = jnp.where(hit1, 1.0, jnp.where(hit2, 1.0, 0.0)).astype(BF16)
        gate = jnp.sum(jnp.where(hit1, rr[2:3, :], jnp.where(hit2, rr[3:4, :], 0.0)),
                       axis=-1, keepdims=True)
        hb = _dot(sel, h_ref[...]).astype(BF16)
        act = (jax.nn.silu(_dot(hb, wg_ref[...])) * _dot(hb, wu_ref[...])).astype(BF16)
        yb = (_dot(act, wd_ref[...]) * gate).astype(BF16)
        rc = rc_ref[...]
        scol = lax.broadcasted_iota(jnp.int32, (tm, MOE_ROWS), 1).astype(F32) + base
        sel_t = jnp.where(rc[:, 0:1] == scol, 1.0,
                          jnp.where(rc[:, 1:2] == scol, 1.0, 0.0)).astype(BF16)
        acc_ref[...] += _dot(sel_t, yb)

    @pl.when(j == pl.num_programs(1) - 1)
    def _():
        xn = x_ref[...] + mod_ref[5] * acc_ref[...]
        y_ref[...] = xn * lax.rsqrt(jnp.mean(xn * xn, axis=-1, keepdims=True) + EPS) * fg_ref[...]


def _moe_tables(cnt, n_blk):
    c = cnt[:, 0, :N_EXPERTS]
    nb_e = jnp.ceil(c / MOE_ROWS).astype(jnp.int32)
    cum = jnp.cumsum(nb_e, axis=1)
    total = cum[:, -1]
    jj = jnp.minimum(jnp.arange(n_blk, dtype=jnp.int32)[None, :], total[:, None] - 1)
    blk_e = jnp.sum((cum[:, None, :] <= jj[:, :, None]).astype(jnp.int32), axis=-1)
    return blk_e.reshape(-1), total


def _moe_routed(h, x, mod, mod_spec2, route, cnt, wg, wu, wd, fg, *, tm):
    n, d = x.shape
    ne, _, ff = wg.shape
    n_tiles = n // tm
    n_blk = (2 * tm + ne * (MOE_ROWS - 1)) // MOE_ROWS
    blk_e, n_used = _moe_tables(cnt, n_blk)
    route_row = route[:, :8].reshape(n_tiles, tm, 8).transpose(0, 2, 1)
    wspec = lambda shape: pl.BlockSpec((None,) + shape, lambda i, j, be, nb: (be[i * n_blk + j], 0, 0))
    once = lambda shape: pl.BlockSpec(shape, lambda i, j, be, nb: (i, 0), pipeline_mode=pl.Buffered(1))
    return pl.pallas_call(
        _moe_routed_kernel,
        grid_spec=pltpu.PrefetchScalarGridSpec(
            num_scalar_prefetch=2,
            grid=(n_tiles, n_blk),
            in_specs=[
                once((tm, d)), once((tm, d)), mod_spec2,
                pl.BlockSpec((tm, LANES), lambda i, j, be, nb: (i, 0)),
                pl.BlockSpec((None, 8, tm), lambda i, j, be, nb: (i, 0, 0)),
                wspec((d, ff)), wspec((d, ff)), wspec((ff, d)),
                pl.BlockSpec((1, d), lambda i, j, be, nb: (0, 0)),
            ],
            out_specs=pl.BlockSpec((tm, d), lambda i, j, be, nb: (i, 0)),
            scratch_shapes=[pltpu.VMEM((tm, d), F32)],
        ),
        out_shape=jax.ShapeDtypeStruct((n, d), F32),
        compiler_params=_cparams("parallel", "arbitrary"),
        name="moe_routed",
    )(blk_e, n_used, h, x, mod, route, route_row, wg, wu, wd, fg)


def _rope_tables(pos):
    half = QK_ROPE // 2
    inv = jnp.exp(-math.log(ROPE_BASE) * jnp.arange(half, dtype=F32) / half)
    ang = pos.astype(F32)[:, None] * inv[None, :]
    cos, sin = jnp.cos(ang), jnp.sin(ang)
    reps = LANES // QK_ROPE
    tc = jnp.tile(jnp.concatenate([cos, cos], axis=-1), (1, reps))
    ts = jnp.tile(jnp.concatenate([-sin, sin], axis=-1), (1, reps))
    return tc, ts


def _swap_halves(w):
    half = w.shape[-1] // 2
    return jnp.concatenate([w[..., half:], w[..., :half]], axis=-1)


def kernel(x_prompt, x_sample, c_prompt, c_sample, state_conv, cache_ckv, cache_kpe, norm1_g, norm2_g, w_ada, b_ada, w_in_ab, b_in_ab, conv_w, conv_b, ln_conv_g, ln_conv_b, ln_v_g, ln_v_b, w_spatial, b_spatial, w_out_ab, b_out_ab, w_ffn_gu, w_ffn_down, w_dc, g_q, g_kv, w_uq, w_uk, w_uv, w_o, w_router, b_router, w_exp_gu, w_exp_down, final_g):
    nb, seq, d = x_prompt.shape
    ns_b, t_s, _ = x_sample.shape
    past = cache_ckv.shape[2]
    c_ch = conv_w.shape[-1]
    n_p = nb * seq
    n_s = ns_b * t_s
    tm = 512
    tq = 1024
    tmoe = 1024
    tiles_per_seq = seq // tm
    s_group = GMLP_CHUNK // t_s

    xp = x_prompt.reshape(n_p, d)
    xs = x_sample.reshape(n_s, d)

    mods = _adaln(jnp.concatenate([c_prompt, c_sample], axis=0), w_ada, b_ada)
    depth = w_ada.shape[0]
    mods = mods.reshape(depth, nb + ns_b, 6, d)

    def prompt_mod(l, k0):
        return mods[l, :nb, k0:k0 + 3][:, :, None, :]

    def sample_mod(l, k0):
        m = mods[l, nb:, k0:k0 + 3]
        m = jnp.broadcast_to(m[:, None], (ns_b, t_s, 3, d)).reshape(n_s, 3, d)
        return m.transpose(1, 0, 2)

    def pspec(tiles_per_b):
        return pl.BlockSpec((None, 3, 1, d), lambda i: (i // tiles_per_b, 0, 0, 0))

    def sspec(rows):
        return pl.BlockSpec((3, rows, d), lambda i: (0, i, 0))

    g1 = norm1_g[0][None]
    g2 = norm2_g[0][None]
    w_in = w_in_ab[0].astype(BF16)
    woa = w_out_ab[0][:c_ch].astype(BF16)
    wob = w_out_ab[0][c_ch:].astype(BF16)
    hd = c_ch // GMLP_HEADS
    bsp_p = jnp.repeat(b_spatial[0].T, hd, axis=1)
    wsp_p = w_spatial[0]
    eye = jnp.eye(s_group, dtype=F32)
    wsp_s = jnp.einsum('ab,hts->hatbs', eye, w_spatial[0][:, :t_s, :t_s]).reshape(
        GMLP_HEADS, GMLP_CHUNK, GMLP_CHUNK)
    bsp_s = jnp.tile(jnp.repeat(b_spatial[0][:, :t_s].T, hd, axis=1), (s_group, 1))
    common = (w_in, b_in_ab[0][None], conv_w[0], conv_b[0][None], ln_conv_g[0][None],
              ln_conv_b[0][None], ln_v_g[0][None], ln_v_b[0][None])
    tail = (woa, wob, b_out_ab[0][None])

    pm = prompt_mod(0, 0)
    xp, conv_p = _mixer(
        xp, pm, pl.BlockSpec((None, 3, 1, d), lambda o, i: (o, 0, 0, 0)), g1,
        common + (wsp_p, bsp_p) + tail, ns=1, t=tm, n_outer=nb, n_inner=tiles_per_seq)
    sm = sample_mod(0, 0)
    n_st = n_s // GMLP_CHUNK
    xs, conv_s, gv_s = _mixer(
        xs, sm, pl.BlockSpec((3, GMLP_CHUNK, d), lambda o, i: (0, o * n_st + i, 0)), g1,
        common + (wsp_s, bsp_s) + tail, ns=s_group, t=t_s, n_outer=1, n_inner=n_st,
        hist=state_conv[0])

    ff = w_ffn_gu.shape[-1] // 2
    wg = w_ffn_gu[0][:, :ff].astype(BF16)
    wu = w_ffn_gu[0][:, ff:].astype(BF16)
    wd = w_ffn_down[0].astype(BF16)
    xp = _ffn(xp, prompt_mod(0, 3), pspec(tiles_per_seq), g2, wg, wu, wd, tm=tm)
    xs = _ffn(xs, sample_mod(0, 3), sspec(n_s), g2, wg, wu, wd, tm=n_s)

    g1 = norm1_g[1][None]
    g2 = norm2_g[1][None]
    scale = (QK_NOPE + QK_ROPE) ** -0.5 * math.log2(math.e)
    zeros64 = jnp.zeros((d, QK_ROPE), F32)
    w_kpe = w_dc[0][:, Q_LORA + KV_LORA:]
    wdc = jnp.concatenate([w_dc[0][:, :Q_LORA + KV_LORA], w_kpe, zeros64,
                           _swap_halves(w_kpe), zeros64], axis=1).astype(BF16)
    wq3 = w_uq[0].reshape(Q_LORA, MLA_HEADS, QK_NOPE + QK_ROPE) * scale
    wq_nope = wq3[:, :, :QK_NOPE]
    wq_pe = wq3[:, :, QK_NOPE:]
    wq_sw = _swap_halves(wq_pe)
    zpad = jnp.zeros((Q_LORA, MLA_HEADS, QK_ROPE), F32)
    wqa = jnp.concatenate([wq_nope, wq_pe, zpad], axis=-1).reshape(Q_LORA, -1).astype(BF16)
    wqb = jnp.concatenate([wq_sw, zpad], axis=-1).reshape(Q_LORA, -1).astype(BF16)
    wuk = w_uk[0].astype(BF16)
    wuvt = w_uv[0].T.astype(BF16)
    gq = g_q[0][None]
    gkv = g_kv[0][None]
    wo = w_o[0].astype(BF16)
    ne = w_router.shape[-1]
    eff = w_exp_gu.shape[-1] // 2
    wr = jnp.zeros((d, LANES), F32).at[:, :ne].set(w_router[0])
    wrh = wr.astype(BF16)
    wrl = (wr - wrh.astype(F32)).astype(BF16)
    br = jnp.full((1, LANES), NEG_BIG, F32).at[0, :ne].set(b_router[0])
    weg = w_exp_gu[0][:, :, :eff].astype(BF16)
    weu = w_exp_gu[0][:, :, eff:].astype(BF16)
    wed = w_exp_down[0].astype(BF16)
    fg = final_g[None]

    tc_p, ts_p = _rope_tables(jnp.arange(seq))
    pm = prompt_mod(1, 0)
    ckv_p, kpe_p, q_p, k_p, vt_p = _mla_prompt(
        xp, pm, pspec(tiles_per_seq), g1, (wdc, gq, gkv, wqa, wqb, wuk, wuvt),
        tc_p, ts_p, tm=tm, tiles_per_seq=tiles_per_seq)
    o_p = _attn_prompt(q_p.reshape(nb, seq, -1), k_p.reshape(nb, seq, -1), vt_p, tq=tq)
    pm6 = mods[1, :nb][:, :, None, :]
    tiles_moe = seq // tmoe
    xp, hp, route_p, cnt_p = _oproj_route(
        xp, pm6, pl.BlockSpec((None, 6, 1, d), lambda i: (i // tiles_moe, 0, 0, 0)),
        o_p.reshape(n_p, -1), wo, g2, wrh, wrl, br, tm=tmoe)
    yp = _moe_routed(
        hp, xp, pm6, pl.BlockSpec((None, 6, 1, d), lambda i, j, be, nb_: (i // tiles_moe, 0, 0, 0)),
        route_p, cnt_p, weg, weu, wed, fg, tm=tmoe)

    tc_s, ts_s = _rope_tables(past + jnp.arange(t_s))
    tc_s = jnp.tile(tc_s, (ns_b, 1))
    ts_s = jnp.tile(ts_s, (ns_b, 1))
    wqn = wq_nope.reshape(Q_LORA, -1).astype(BF16)
    wqp = wq_pe.reshape(Q_LORA, -1).astype(BF16)
    wqs = wq_sw.reshape(Q_LORA, -1).astype(BF16)
    wukt = w_uk[0].reshape(KV_LORA, MLA_HEADS, QK_NOPE).transpose(1, 2, 0).astype(BF16)
    wuv_h = w_uv[0].reshape(KV_LORA, MLA_HEADS, V_HD).transpose(1, 0, 2).astype(BF16)
    sm = sample_mod(1, 0)
    ckv_s, kpe_s, qlat, qpe = _mla_sample(
        xs, sm, sspec(n_s), g1, (wdc, gq, gkv, wqn, wqp, wqs, wukt), tc_s, ts_s, tm=n_s)
    o_lat = _attn_sample(
        qlat.reshape(ns_b, t_s * MLA_HEADS, KV_LORA), qpe.reshape(ns_b, t_s * MLA_HEADS, QK_ROPE),
        cache_ckv[0], cache_kpe[0], ckv_s.reshape(ns_b, t_s, KV_LORA),
        kpe_s.reshape(ns_b, t_s, QK_ROPE))
    sm6 = jnp.broadcast_to(mods[1, nb:][:, None], (ns_b, t_s, 6, d)).reshape(n_s, 6, d).transpose(1, 0, 2)
    xs, hs, route_s, cnt_s = _oproj_route(
        xs, sm6, pl.BlockSpec((6, n_s, d), lambda i: (0, i, 0)),
        o_lat.reshape(n_s, MLA_HEADS * KV_LORA), wo, g2, wrh, wrl, br, wuv_h, tm=n_s)
    ys = _moe_routed(
        hs, xs, sm6, pl.BlockSpec((6, n_s, d), lambda i, j, be, nb_: (0, i, 0)),
        route_s, cnt_s, weg, weu, wed, fg, tm=n_s)

    return (yp.reshape(nb, seq, d), ys.reshape(ns_b, t_s, d),
            conv_p[None], conv_s[None], gv_s.reshape(ns_b, t_s, c_ch)[None],
            ckv_p.reshape(nb, seq, KV_LORA)[None], kpe_p.reshape(nb, seq, QK_ROPE)[None],
            ckv_s.reshape(ns_b, t_s, KV_LORA)[None], kpe_s.reshape(ns_b, t_s, QK_ROPE)[None])
```

```python
import functools
import math

import jax
import jax.numpy as jnp
from jax import lax
from jax.experimental import pallas as pl
from jax.experimental.pallas import tpu as pltpu

F32 = jnp.float32
BF16 = jnp.bfloat16

EPS = 1e-6
CHUNK = 64
CONV_W = 31
HIST = CONV_W - 1
HIST_PAD = 32
GMLP_CHUNK = 128
GMLP_HEADS = 8
MLA_HEADS = 8
Q_LORA = 256
KV_LORA = 256
QK_NOPE = 128
QK_ROPE = 64
V_HD = 128
QK_CAT = 256
ROPE_BASE = 10000.0
N_EXPERTS = 8
MOE_ROWS = 256
MOE_PAD = 128
LANES = 128
SUBLANES = 8
NEG_BIG = -1e30

VMEM_LIMIT = 56 * 1024 * 1024


def _cparams(*sem):
    return pltpu.CompilerParams(dimension_semantics=sem, vmem_limit_bytes=VMEM_LIMIT)


def _const_spec(shape):
    nd = len(shape)
    return pl.BlockSpec(shape, lambda *_: (0,) * nd, pipeline_mode=pl.Buffered(1))


def _modulate(x, g, shift, scale):
    y = x * lax.rsqrt(jnp.mean(x * x, axis=-1, keepdims=True) + EPS)
    return (y * g) * (1.0 + scale) + shift


def _layer_norm(x, g, b):
    mu = jnp.mean(x, axis=-1, keepdims=True)
    xc = x - mu
    var = jnp.mean(xc * xc, axis=-1, keepdims=True)
    return xc * lax.rsqrt(var + EPS) * g + b


def _dot(a, b):
    return jnp.dot(a, b, preferred_element_type=F32)


_NT = (((1,), (1,)), ((), ()))


def _adaln_kernel(c_ref, w_ref, b_ref, o_ref):
    c = c_ref[...]
    a = jax.nn.silu(c).astype(BF16)
    o_ref[...] = _dot(a, w_ref[...].astype(BF16)) + b_ref[...]


def _adaln(c_all, w_ada, b_ada, tn=1536):
    depth, d, n = w_ada.shape
    rows = c_all.shape[0]
    return pl.pallas_call(
        _adaln_kernel,
        grid=(depth, n // tn),
        in_specs=[
            pl.BlockSpec((rows, d), lambda l, j: (0, 0)),
            pl.BlockSpec((None, d, tn), lambda l, j: (l, 0, j)),
            pl.BlockSpec((None, 1, tn), lambda l, j: (l, 0, j)),
        ],
        out_specs=pl.BlockSpec((None, rows, tn), lambda l, j: (l, 0, j)),
        out_shape=jax.ShapeDtypeStruct((depth, rows, n), F32),
        compiler_params=_cparams("parallel", "parallel"),
        name="adaln",
    )(c_all, w_ada, b_ada.reshape(depth, 1, n))


def _mixer_kernel(*refs, ns, t, carry_hist, conv_rows):
    if carry_hist:
        (x_ref, mod_ref, g_ref, win_ref, bin_ref, cw_ref, cb_ref, lag_ref, lab_ref,
         lvg_ref, lvb_ref, wsp_ref, bsp_ref, woa_ref, wob_ref, bout_ref,
         xo_ref, conv_ref, aext_ref, ya_ref, yb_ref, tapwin_ref) = refs
        hist_ref = v_ref = None
    else:
        (x_ref, mod_ref, g_ref, win_ref, bin_ref, cw_ref, cb_ref, lag_ref, lab_ref,
         lvg_ref, lvb_ref, wsp_ref, bsp_ref, woa_ref, wob_ref, bout_ref, hist_ref,
         xo_ref, conv_ref, v_ref, aext_ref, ya_ref, yb_ref, tapwin_ref) = refs
    c = cw_ref.shape[1]
    rows = ns * t

    x = x_ref[...]
    h = _modulate(x, g_ref[...], mod_ref[0], mod_ref[1]).astype(BF16)
    p = _dot(h, win_ref[...]) + bin_ref[...]

    a = p[:, :c] * jax.nn.sigmoid(p[:, c:2 * c])
    if carry_hist:
        @pl.when(pl.program_id(1) == 0)
        def _():
            aext_ref[:, :HIST_PAD, :] = jnp.zeros((ns, HIST_PAD, c), F32)
    else:
        aext_ref[:, HIST_PAD - HIST:HIST_PAD, :] = hist_ref[...]
    aext_ref[:, HIST_PAD:, :] = a.reshape(ns, t, c)

    off = HIST_PAD - HIST
    n_cc = t // conv_rows
    lag = lag_ref[...]
    lab = lab_ref[...]
    cb = cb_ref[...]

    def finish(acc):
        y = _layer_norm(acc + cb, lag, lab)
        return jax.nn.silu(y).astype(BF16)

    def conv_direct(s):
        acc = jnp.zeros((conv_rows, c), F32)
        for k in range(CONV_W):
            acc = acc + aext_ref[s, off + k:off + k + conv_rows, :] * cw_ref[k:k + 1, :]
        return finish(acc)

    def conv_phased(s, r0):
        win = conv_rows + HIST_PAD
        span = win - SUBLANES
        tapwin_ref[0] = aext_ref[s, pl.ds(r0, win), :]
        for r in range(1, SUBLANES):
            tapwin_ref[r, :span, :] = tapwin_ref[0, r:r + span, :]
        acc = jnp.zeros((conv_rows, c), F32)
        for k in range(CONV_W):
            r = (off + k) % SUBLANES
            q8 = off + k - r
            acc = acc + tapwin_ref[r, q8:q8 + conv_rows, :] * cw_ref[k:k + 1, :]
        return finish(acc)

    for s in range(ns):
        if n_cc == 1:
            ya_ref[s * t:(s + 1) * t, :] = conv_direct(s)
        else:
            def body(i, carry, s=s):
                r0 = pl.multiple_of(i * conv_rows, conv_rows)
                ya_ref[pl.ds(s * t + r0, conv_rows), :] = conv_phased(s, r0)
                return carry
            lax.fori_loop(0, n_cc, body, 0)

    conv_ref[...] = aext_ref[:, t + HIST_PAD - HIST:t + HIST_PAD, :]
    if carry_hist:
        aext_ref[:, :HIST_PAD, :] = aext_ref[:, t:t + HIST_PAD, :]

    z = jax.nn.gelu(p[:, 2 * c:])
    u = z[:, :c]
    v = _layer_norm(z[:, c:], lvg_ref[...], lvb_ref[...])
    if v_ref is not None:
        v_ref[...] = v
    ri = lax.broadcasted_iota(jnp.int32, (GMLP_CHUNK, GMLP_CHUNK), 0)
    ci = lax.broadcasted_iota(jnp.int32, (GMLP_CHUNK, GMLP_CHUNK), 1)
    tril = ri >= ci
    low_half = ci < (LANES // 2)
    w_heads = [jnp.where(tril, wsp_ref[hh], 0.0).astype(BF16) for hh in range(GMLP_HEADS)]
    for blk in range(rows // GMLP_CHUNK):
        r = slice(blk * GMLP_CHUNK, (blk + 1) * GMLP_CHUNK)
        for j in range(c // LANES):
            l = slice(j * LANES, (j + 1) * LANES)
            vp = v[r, l]
            lo = jnp.where(low_half, vp, 0.0).astype(BF16)
            hi = jnp.where(low_half, 0.0, vp).astype(BF16)
            sp = _dot(w_heads[2 * j], lo) + _dot(w_heads[2 * j + 1], hi)
            yb_ref[r, l] = (u[r, l] * (sp + bsp_ref[:, l])).astype(BF16)

    out = _dot(ya_ref[...], woa_ref[...]) + _dot(yb_ref[...], wob_ref[...]) + bout_ref[...]
    xo_ref[...] = x + mod_ref[2] * out


def _mixer(x, mod, mod_spec, g1, wts, *, ns, t, n_outer, n_inner, hist=None):
    (w_in, b_in, cw, cb, lag, lab, lvg, lvb, wsp, bsp, woa, wob, b_out) = wts
    d = x.shape[1]
    c = cw.shape[1]
    rows = ns * t
    carry_hist = hist is None
    tile = lambda o, i: (o * n_inner + i, 0)
    in_specs = [
        pl.BlockSpec((rows, d), tile),
        mod_spec,
        _const_spec((1, d)),
        _const_spec(w_in.shape), _const_spec(b_in.shape),
        _const_spec(cw.shape), _const_spec(cb.shape),
        _const_spec(lag.shape), _const_spec(lab.shape),
        _const_spec(lvg.shape), _const_spec(lvb.shape),
        _const_spec(wsp.shape), _const_spec(bsp.shape),
        _const_spec(woa.shape), _const_spec(wob.shape), _const_spec(b_out.shape),
    ]
    args = [x, mod, g1, w_in, b_in, cw, cb, lag, lab, lvg, lvb, wsp, bsp, woa, wob, b_out]
    n_tiles = n_outer * n_inner
    out_shape = [jax.ShapeDtypeStruct(x.shape, F32)]
    out_specs = [pl.BlockSpec((rows, d), tile)]
    if carry_hist:
        out_shape.append(jax.ShapeDtypeStruct((n_outer * ns, HIST, c), F32))
        out_specs.append(pl.BlockSpec((ns, HIST, c), lambda o, i: (o, 0, 0)))
    else:
        in_specs.append(pl.BlockSpec((ns, HIST, c), lambda o, i: (o * n_inner + i, 0, 0)))
        args.append(hist)
        out_shape.append(jax.ShapeDtypeStruct((n_tiles * ns, HIST, c), F32))
        out_specs.append(pl.BlockSpec((ns, HIST, c), lambda o, i: (o * n_inner + i, 0, 0)))
        out_shape.append(jax.ShapeDtypeStruct((x.shape[0], c), F32))
        out_specs.append(pl.BlockSpec((rows, c), tile))
    conv_rows = min(t, 64)
    return pl.pallas_call(
        functools.partial(_mixer_kernel, ns=ns, t=t, carry_hist=carry_hist, conv_rows=conv_rows),
        grid=(n_outer, n_inner),
        in_specs=in_specs,
        out_specs=out_specs,
        out_shape=out_shape,
        scratch_shapes=[
            pltpu.VMEM((ns, t + HIST_PAD, c), F32),
            pltpu.VMEM((rows, c), BF16),
            pltpu.VMEM((rows, c), BF16),
            pltpu.VMEM((SUBLANES, conv_rows + HIST_PAD, c), F32),
        ],
        compiler_params=_cparams("arbitrary", "arbitrary"),
        name="mixer_prompt" if carry_hist else "mixer_sample",
    )(*args)


def _ffn_kernel(x_ref, mod_ref, g_ref, wg_ref, wu_ref, wd_ref, o_ref, *, n_chunks):
    x = x_ref[...]
    h = _modulate(x, g_ref[...], mod_ref[0], mod_ref[1]).astype(BF16)
    ff = wg_ref.shape[1]
    cw = ff // n_chunks
    acc = None
    for j in range(n_chunks):
        sl = slice(j * cw, (j + 1) * cw)
        gate = _dot(h, wg_ref[:, sl])
        up = _dot(h, wu_ref[:, sl])
        act = (jax.nn.silu(gate) * up).astype(BF16)
        part = _dot(act, wd_ref[sl, :])
        acc = part if acc is None else acc + part
    o_ref[...] = x + mod_ref[2] * acc


def _ffn(x, mod, mod_spec, g2, wg, wu, wd, *, tm):
    n, d = x.shape
    return pl.pallas_call(
        functools.partial(_ffn_kernel, n_chunks=2),
        grid=(n // tm,),
        in_specs=[
            pl.BlockSpec((tm, d), lambda i: (i, 0)),
            mod_spec,
            _const_spec((1, d)),
            _const_spec(wg.shape), _const_spec(wu.shape), _const_spec(wd.shape),
        ],
        out_specs=pl.BlockSpec((tm, d), lambda i: (i, 0)),
        out_shape=jax.ShapeDtypeStruct((n, d), F32),
        compiler_params=_cparams("parallel"),
        name="ffn",
    )(x, mod, g2, wg, wu, wd)


def _mla_latents(x_ref, mod_ref, g_ref, wdc_ref, gq_ref, gkv_ref, tc, ts):
    x = x_ref[...]
    h = _modulate(x, g_ref[...], mod_ref[0], mod_ref[1]).astype(BF16)
    p = _dot(h, wdc_ref[...])
    cq = p[:, :Q_LORA]
    cq = cq * lax.rsqrt(jnp.mean(cq * cq, axis=-1, keepdims=True) + EPS) * gq_ref[...]
    ckv = p[:, Q_LORA:Q_LORA + KV_LORA]
    ckv = ckv * lax.rsqrt(jnp.mean(ckv * ckv, axis=-1, keepdims=True) + EPS) * gkv_ref[...]
    b0 = Q_LORA + KV_LORA
    kpe = p[:, b0:b0 + LANES] * tc + p[:, b0 + LANES:b0 + 2 * LANES] * ts
    return cq.astype(BF16), ckv, kpe


def _mla_prompt_kernel(x_ref, mod_ref, g_ref, wdc_ref, gq_ref, gkv_ref, tc_ref, ts_ref,
                       wqa_ref, wqb_ref, wuk_ref, wuvt_ref,
                       ckv_ref, kpe_ref, q_ref, k_ref, vt_ref):
    tc = tc_ref[...]
    ts = ts_ref[...]
    cq, ckv, kpe = _mla_latents(x_ref, mod_ref, g_ref, wdc_ref, gq_ref, gkv_ref, tc, ts)
    ckv_ref[...] = ckv
    kpe_ref[...] = kpe[:, :QK_ROPE]
    qa = _dot(cq, wqa_ref[...])
    qb = _dot(cq, wqb_ref[...])
    ckv_b = ckv.astype(BF16)
    kn = _dot(ckv_b, wuk_ref[...])
    vt_ref[...] = lax.dot_general(wuvt_ref[...], ckv_b, _NT, preferred_element_type=F32).astype(BF16)
    kpe_b = kpe.astype(BF16)
    for hh in range(MLA_HEADS):
        base = hh * QK_CAT
        q_ref[:, base:base + QK_NOPE] = qa[:, base:base + QK_NOPE].astype(BF16)
        q_ref[:, base + QK_NOPE:base + QK_CAT] = (
            qa[:, base + QK_NOPE:base + QK_CAT] * tc
            + qb[:, hh * LANES:(hh + 1) * LANES] * ts).astype(BF16)
        k_ref[:, base:base + QK_NOPE] = kn[:, hh * QK_NOPE:(hh + 1) * QK_NOPE].astype(BF16)
        k_ref[:, base + QK_NOPE:base + QK_CAT] = kpe_b


def _mla_prompt(x, mod, mod_spec, g1, wts, tc, ts, *, tm, tiles_per_seq):
    wdc, gq, gkv, wqa, wqb, wuk, wuvt = wts
    n, d = x.shape
    hq = MLA_HEADS * QK_CAT
    hv = MLA_HEADS * V_HD
    nb = n // (tm * tiles_per_seq)
    row = lambda i: (i, 0)
    pos = lambda i: (i % tiles_per_seq, 0)
    return pl.pallas_call(
        _mla_prompt_kernel,
        grid=(n // tm,),
        in_specs=[
            pl.BlockSpec((tm, d), row), mod_spec, _const_spec((1, d)),
            _const_spec(wdc.shape), _const_spec(gq.shape), _const_spec(gkv.shape),
            pl.BlockSpec((tm, LANES), pos), pl.BlockSpec((tm, LANES), pos),
            _const_spec(wqa.shape), _const_spec(wqb.shape),
            _const_spec(wuk.shape), _const_spec(wuvt.shape),
        ],
        out_specs=[
            pl.BlockSpec((tm, KV_LORA), row), pl.BlockSpec((tm, QK_ROPE), row),
            pl.BlockSpec((tm, hq), row), pl.BlockSpec((tm, hq), row),
            pl.BlockSpec((None, hv, tm), lambda i: (i // tiles_per_seq, 0, i % tiles_per_seq)),
        ],
        out_shape=[
            jax.ShapeDtypeStruct((n, KV_LORA), F32), jax.ShapeDtypeStruct((n, QK_ROPE), F32),
            jax.ShapeDtypeStruct((n, hq), BF16), jax.ShapeDtypeStruct((n, hq), BF16),
            jax.ShapeDtypeStruct((nb, hv, tm * tiles_per_seq), BF16),
        ],
        compiler_params=_cparams("parallel"),
        name="mla_proj_prompt",
    )(x, mod, g1, wdc, gq, gkv, tc, ts, wqa, wqb, wuk, wuvt)


def _mla_sample_kernel(x_ref, mod_ref, g_ref, wdc_ref, gq_ref, gkv_ref, tc_ref, ts_ref,
                       wqn_ref, wqp_ref, wqs_ref, wukt_ref,
                       ckv_ref, kpe_ref, qlat_ref, qpe_ref):
    tc = tc_ref[...]
    ts = ts_ref[...]
    cq, ckv, kpe = _mla_latents(x_ref, mod_ref, g_ref, wdc_ref, gq_ref, gkv_ref, tc, ts)
    ckv_ref[...] = ckv
    kpe_ref[...] = kpe[:, :QK_ROPE]
    qn = _dot(cq, wqn_ref[...]).astype(BF16)
    qp = _dot(cq, wqp_ref[...])
    qs = _dot(cq, wqs_ref[...])
    for j in range(MLA_HEADS * QK_ROPE // LANES):
        l = slice(j * LANES, (j + 1) * LANES)
        qpe_ref[:, l] = (qp[:, l] * tc + qs[:, l] * ts).astype(BF16)
    for hh in range(MLA_HEADS):
        qlat_ref[:, hh * KV_LORA:(hh + 1) * KV_LORA] = _dot(
            qn[:, hh * QK_NOPE:(hh + 1) * QK_NOPE], wukt_ref[hh]).astype(BF16)


def _mla_sample(x, mod, mod_spec, g1, wts, tc, ts, *, tm):
    wdc, gq, gkv, wqn, wqp, wqs, wukt = wts
    n, d = x.shape
    row = lambda i: (i, 0)
    return pl.pallas_call(
        _mla_sample_kernel,
        grid=(n // tm,),
        in_specs=[
            pl.BlockSpec((tm, d), row), mod_spec, _const_spec((1, d)),
            _const_spec(wdc.shape), _const_spec(gq.shape), _const_spec(gkv.shape),
            pl.BlockSpec((tm, LANES), row), pl.BlockSpec((tm, LANES), row),
            _const_spec(wqn.shape), _const_spec(wqp.shape), _const_spec(wqs.shape),
            _const_spec(wukt.shape),
        ],
        out_specs=[
            pl.BlockSpec((tm, KV_LORA), row), pl.BlockSpec((tm, QK_ROPE), row),
            pl.BlockSpec((tm, MLA_HEADS * KV_LORA), row),
            pl.BlockSpec((tm, MLA_HEADS * QK_ROPE), row),
        ],
        out_shape=[
            jax.ShapeDtypeStruct((n, KV_LORA), F32), jax.ShapeDtypeStruct((n, QK_ROPE), F32),
            jax.ShapeDtypeStruct((n, MLA_HEADS * KV_LORA), BF16),
            jax.ShapeDtypeStruct((n, MLA_HEADS * QK_ROPE), BF16),
        ],
        compiler_params=_cparams("parallel"),
        name="mla_proj_sample",
    )(x, mod, g1, wdc, gq, gkv, tc, ts, wqn, wqp, wqs, wukt)


def _attn_prompt_kernel(q_ref, k_ref, vt_ref, o_ref, s_scr, p_scr, acc_scr, *, tq, tk):
    qi = pl.program_id(2)
    n_full = 2 * qi

    def qk_to(slot, t):
        k = k_ref[pl.ds(pl.multiple_of(t * tk, tk), tk), :]
        s = lax.dot_general(k, q_ref[...], _NT, preferred_element_type=F32)
        s_scr[slot] = s
        return jnp.max(s, axis=0, keepdims=True)

    def softmax_from(slot, m, l, bm):
        m_new = jnp.maximum(m, bm)
        alpha = jnp.exp2(m - m_new)
        p = jnp.exp2(s_scr[slot] - m_new)
        p_scr[slot] = p.astype(BF16)
        l = alpha * l + jnp.sum(p, axis=0, keepdims=True)
        return m_new, l, alpha

    def pv_from(slot, t, alpha):
        vt = vt_ref[:, pl.ds(pl.multiple_of(t * tk, tk), tk)]
        acc_scr[...] = alpha * acc_scr[...] + _dot(vt, p_scr[slot])

    p_scr[1] = jnp.zeros((tk, tq), BF16)
    acc_scr[...] = jnp.zeros((V_HD, tq), F32)
    m0 = jnp.full((1, tq), NEG_BIG, F32)
    l0 = jnp.zeros((1, tq), F32)
    bm0 = qk_to(0, 0)

    def body(i, carry):
        m, l, bm, a_prev = carry
        t = 2 * i
        bm1 = qk_to(1, t + 1)
        m, l, a0 = softmax_from(0, m, l, bm)
        pv_from(1, jnp.maximum(t - 1, 0), a_prev)
        bm2 = qk_to(0, t + 2)
        m, l, a1 = softmax_from(1, m, l, bm1)
        pv_from(0, t, a0)
        return m, l, bm2, a1

    m, l, bm, a_prev = lax.fori_loop(0, qi, body, (m0, l0, bm0, jnp.ones((1, tq), F32)))
    pv_from(1, jnp.maximum(n_full - 1, 0), a_prev)
    kk = lax.broadcasted_iota(jnp.int32, (tk, tk), 0) // CHUNK
    qq = lax.broadcasted_iota(jnp.int32, (tk, tk), 1) // CHUNK
    vis = qq >= kk
    s_a = jnp.where(vis, s_scr[0, :, :tk], NEG_BIG)
    bm_d = jnp.concatenate([jnp.max(s_a, axis=0, keepdims=True), bm[:, tk:]], axis=1)
    m_new = jnp.maximum(m, bm_d)
    alpha = jnp.exp2(m - m_new)
    p_a = jnp.exp2(s_a - m_new[:, :tk])
    p_b = jnp.exp2(s_scr[0, :, tk:] - m_new[:, tk:])
    p_scr[0, :, :tk] = p_a.astype(BF16)
    p_scr[0, :, tk:] = p_b.astype(BF16)
    l = alpha * l + jnp.concatenate([jnp.sum(p_a, axis=0, keepdims=True),
                                     jnp.sum(p_b, axis=0, keepdims=True)], axis=1)
    pv_from(0, n_full, alpha)
    k1 = k_ref[pl.ds(pl.multiple_of((n_full + 1) * tk, tk), tk), :]
    s1 = lax.dot_general(k1, q_ref[tk:, :], _NT, preferred_element_type=F32)
    s1 = jnp.where(vis, s1, NEG_BIG)
    m_b = m_new[:, tk:]
    m_b2 = jnp.maximum(m_b, jnp.max(s1, axis=0, keepdims=True))
    a1 = jnp.exp2(m_b - m_b2)
    p1 = jnp.exp2(s1 - m_b2)
    l_b = a1 * l[:, tk:] + jnp.sum(p1, axis=0, keepdims=True)
    vt1 = vt_ref[:, pl.ds(pl.multiple_of((n_full + 1) * tk, tk), tk)]
    acc_b = a1 * acc_scr[:, tk:] + _dot(vt1, p1.astype(BF16))
    o_ref[:tk, :] = (acc_scr[:, :tk] / l[:, :tk]).T.astype(o_ref.dtype)
    o_ref[tk:, :] = (acc_b / l_b).T.astype(o_ref.dtype)


def _attn_prompt(q, k, vt, *, tq):
    b, s, _ = q.shape
    tk = tq // 2
    return pl.pallas_call(
        functools.partial(_attn_prompt_kernel, tq=tq, tk=tk),
        grid=(b, MLA_HEADS, s // tq),
        in_specs=[
            pl.BlockSpec((None, tq, QK_CAT), lambda bi, hi, qi: (bi, qi, hi)),
            pl.BlockSpec((None, s, QK_CAT), lambda bi, hi, qi: (bi, 0, hi)),
            pl.BlockSpec((None, V_HD, s), lambda bi, hi, qi: (bi, hi, 0)),
        ],
        out_specs=pl.BlockSpec((None, tq, V_HD), lambda bi, hi, qi: (bi, qi, hi)),
        out_shape=jax.ShapeDtypeStruct((b, s, MLA_HEADS * V_HD), BF16),
        scratch_shapes=[pltpu.VMEM((2, tk, tq), F32), pltpu.VMEM((2, tk, tq), BF16),
                        pltpu.VMEM((V_HD, tq), F32)],
        compiler_params=_cparams("parallel", "parallel", "arbitrary"),
        name="attn_prompt",
    )(q, k, vt)


def _attn_sample_kernel(ql_ref, qp_ref, cc_ref, cp_ref, nc_ref, np_ref, o_ref):
    ql = ql_ref[...]
    qp = qp_ref[...]
    nt = (((1,), (1,)), ((), ()))
    cc = cc_ref[...].astype(BF16)
    cp = cp_ref[...].astype(BF16)
    nc = nc_ref[...].astype(BF16)
    npe = np_ref[...].astype(BF16)
    s_c = (lax.dot_general(ql, cc, nt, preferred_element_type=F32)
           + lax.dot_general(qp, cp, nt, preferred_element_type=F32))
    s_n = (lax.dot_general(ql, nc, nt, preferred_element_type=F32)
           + lax.dot_general(qp, npe, nt, preferred_element_type=F32))
    m = jnp.maximum(jnp.max(s_c, axis=-1, keepdims=True), jnp.max(s_n, axis=-1, keepdims=True))
    p_c = jnp.exp2(s_c - m)
    p_n = jnp.exp2(s_n - m)
    l = jnp.sum(p_c, axis=-1, keepdims=True) + jnp.sum(p_n, axis=-1, keepdims=True)
    o = _dot(p_c.astype(BF16), cc) + _dot(p_n.astype(BF16), nc)
    o_ref[...] = (o / l).astype(o_ref.dtype)


def _attn_sample(qlat, qpe, cache_ckv, cache_kpe, ckv_new, kpe_new):
    nb, past, _ = cache_ckv.shape
    r = qlat.shape[1]
    t = ckv_new.shape[1]
    blk = lambda shape: pl.BlockSpec((None,) + shape, lambda i: (i, 0, 0))
    return pl.pallas_call(
        _attn_sample_kernel,
        grid=(nb,),
        in_specs=[blk((r, KV_LORA)), blk((r, QK_ROPE)), blk((past, KV_LORA)),
                  blk((past, QK_ROPE)), blk((t, KV_LORA)), blk((t, QK_ROPE))],
        out_specs=blk((r, KV_LORA)),
        out_shape=jax.ShapeDtypeStruct((nb, r, KV_LORA), BF16),
        compiler_params=_cparams("parallel"),
        name="attn_sample",
    )(qlat, qpe, cache_ckv, cache_kpe, ckv_new, kpe_new)


def _route_tail(xm, mod_ref, g_ref, wrh_ref, wrl_ref, br_ref, xo_ref, h_ref, route_ref, cnt_ref):
    tm = xm.shape[0]
    xo_ref[...] = xm
    hf = _modulate(xm, g_ref[...], mod_ref[3], mod_ref[4])
    hb = hf.astype(BF16)
    h_ref[...] = hb
    h_lo = (hf - hb.astype(F32)).astype(BF16)
    logits = (_dot(hb, wrh_ref[...]) + _dot(hb, wrl_ref[...])
              + _dot(h_lo, wrh_ref[...])) + br_ref[...]
    lane = lax.broadcasted_iota(jnp.int32, (tm, LANES), 1)
    m1 = jnp.max(logits, axis=-1, keepdims=True)
    i1 = jnp.min(jnp.where(logits == m1, lane, LANES), axis=-1, keepdims=True)
    rest = jnp.where(lane == i1, -jnp.inf, logits)
    m2 = jnp.max(rest, axis=-1, keepdims=True)
    i2 = jnp.min(jnp.where(rest == m2, lane, LANES), axis=-1, keepdims=True)
    e2 = jnp.exp(m2 - m1)
    den = 1.0 + e2
    sel1 = lane == i1
    sel2 = lane == i2
    onehot = jnp.where(sel1, 1.0, jnp.where(sel2, 1.0, 0.0))
    ri = lax.broadcasted_iota(jnp.int32, (tm, tm), 0)
    ci = lax.broadcasted_iota(jnp.int32, (tm, tm), 1)
    before = jnp.where(ri > ci, 1.0, 0.0).astype(BF16)
    rank = _dot(before, onehot.astype(BF16))
    cnt = jnp.sum(onehot, axis=0, keepdims=True)
    nblk = jnp.floor((cnt + (MOE_PAD - 1)) * (1.0 / MOE_PAD))
    r8 = lax.broadcasted_iota(jnp.int32, (LANES, LANES), 0)
    c8 = lax.broadcasted_iota(jnp.int32, (LANES, LANES), 1)
    upper = jnp.where(r8 < c8, 1.0, 0.0).astype(BF16)
    first_blk = _dot(jnp.broadcast_to(nblk, (8, LANES)).astype(BF16), upper)[0:1]
    slot = first_blk * MOE_PAD + rank
    d1 = jnp.sum(jnp.where(sel1, slot, 0.0), axis=-1, keepdims=True)
    d2 = jnp.sum(jnp.where(sel2, slot, 0.0), axis=-1, keepdims=True)
    route_ref[...] = jnp.where(lane == 0, d1, jnp.where(lane == 1, d2, jnp.where(
        lane == 2, 1.0 / den, jnp.where(lane == 3, e2 / den, 0.0))))
    cnt_ref[...] = jnp.broadcast_to(cnt, (8, LANES))


def _oproj_route_kernel(x_ref, mod_ref, o_ref, wo_ref, g_ref, wrh_ref, wrl_ref, br_ref,
                        xo_ref, h_ref, route_ref, cnt_ref):
    xm = x_ref[...] + mod_ref[2] * _dot(o_ref[...], wo_ref[...])
    _route_tail(xm, mod_ref, g_ref, wrh_ref, wrl_ref, br_ref, xo_ref, h_ref, route_ref, cnt_ref)


def _oproj_latent_route_kernel(x_ref, mod_ref, o_ref, wuv_ref, wo_ref, g_ref, wrh_ref, wrl_ref,
                               br_ref, xo_ref, h_ref, route_ref, cnt_ref):
    acc = None
    for hh in range(MLA_HEADS):
        oh = _dot(o_ref[:, hh * KV_LORA:(hh + 1) * KV_LORA], wuv_ref[hh]).astype(BF16)
        part = _dot(oh, wo_ref[hh * V_HD:(hh + 1) * V_HD, :])
        acc = part if acc is None else acc + part
    xm = x_ref[...] + mod_ref[2] * acc
    _route_tail(xm, mod_ref, g_ref, wrh_ref, wrl_ref, br_ref, xo_ref, h_ref, route_ref, cnt_ref)


def _oproj_route(x, mod, mod_spec, o, wo, g2, wrh, wrl, br, wuv=None, *, tm):
    n, d = x.shape
    row = lambda i: (i, 0)
    in_specs = [pl.BlockSpec((tm, d), row), mod_spec, pl.BlockSpec((tm, o.shape[1]), row)]
    args = [x, mod, o]
    if wuv is not None:
        in_specs.append(_const_spec(wuv.shape))
        args.append(wuv)
    in_specs += [_const_spec(wo.shape), _const_spec((1, d)), _const_spec(wrh.shape),
                 _const_spec(wrl.shape), _const_spec(br.shape)]
    args += [wo, g2, wrh, wrl, br]
    return pl.pallas_call(
        _oproj_route_kernel if wuv is None else _oproj_latent_route_kernel,
        grid=(n // tm,),
        in_specs=in_specs,
        out_specs=[pl.BlockSpec((tm, d), row), pl.BlockSpec((tm, d), row),
                   pl.BlockSpec((tm, LANES), row), pl.BlockSpec((None, 8, LANES), lambda i: (i, 0, 0))],
        out_shape=[jax.ShapeDtypeStruct((n, d), F32), jax.ShapeDtypeStruct((n, d), BF16),
                   jax.ShapeDtypeStruct((n, LANES), F32),
                   jax.ShapeDtypeStruct((n // tm, 8, LANES), F32)],
        compiler_params=_cparams("parallel"),
        name="oproj_route" if wuv is None else "oproj_latent_route",
    )(*args)


_BLK_SKIP, _BLK_FULL, _BLK_HALF = 0, 1, 2


def _moe_routed_kernel(be_ref, row0_ref, kind_ref, h_ref, x_ref, mod_ref, rc_ref, rr_ref,
                       wg_ref, wu_ref, wd_ref, fg_ref, y_ref, acc_ref, *, n_blk):
    i = pl.program_id(0)
    j = pl.program_id(1)
    tm = h_ref.shape[0]
    kind = kind_ref[i * n_blk + j]

    @pl.when(j == 0)
    def _():
        acc_ref[...] = jnp.zeros(acc_ref.shape, F32)

    def block(rows):
        base = row0_ref[i * n_blk + j].astype(F32)
        rr = rr_ref[...]
        srow = lax.broadcasted_iota(jnp.int32, (rows, tm), 0).astype(F32) + base
        hit1 = rr[0:1, :] == srow
        hit2 = rr[1:2, :] == srow
        sel = jnp.where(hit1, 1.0, jnp.where(hit2, 1.0, 0.0)).astype(BF16)
        gate = jnp.sum(jnp.where(hit1, rr[2:3, :], jnp.where(hit2, rr[3:4, :], 0.0)),
                       axis=-1, keepdims=True)
        hb = _dot(sel, h_ref[...]).astype(BF16)
        act = (jax.nn.silu(_dot(hb, wg_ref[...])) * _dot(hb, wu_ref[...])).astype(BF16)
        yb = (_dot(act, wd_ref[...]) * gate).astype(BF16)
        rc = rc_ref[...]
        scol = lax.broadcasted_iota(jnp.int32, (tm, rows), 1).astype(F32) + base
        sel_t = jnp.where(rc[:, 0:1] == scol, 1.0,
                          jnp.where(rc[:, 1:2] == scol, 1.0, 0.0)).astype(BF16)
        acc_ref[...] += _dot(sel_t, yb)

    @pl.when(kind == _BLK_FULL)
    def _():
        block(MOE_ROWS)

    @pl.when(kind == _BLK_HALF)
    def _():
        block(MOE_PAD)

    @pl.when(j == pl.num_programs(1) - 1)
    def _():
        xn = x_ref[...] + mod_ref[5] * acc_ref[...]
        y_ref[...] = xn * lax.rsqrt(jnp.mean(xn * xn, axis=-1, keepdims=True) + EPS) * fg_ref[...]


def _moe_tables(cnt, n_blk):
    c = cnt[:, 0, :N_EXPERTS]
    n_pad = jnp.ceil(c / MOE_PAD).astype(jnp.int32)
    per = MOE_ROWS // MOE_PAD
    nb_e = (n_pad + per - 1) // per
    cum = jnp.cumsum(nb_e, axis=1)
    total = cum[:, -1:]
    j = jnp.arange(n_blk, dtype=jnp.int32)[None, :]
    jj = jnp.minimum(j, total - 1)
    blk_e = jnp.sum((cum[:, None, :] <= jj[:, :, None]).astype(jnp.int32), axis=-1)
    take = lambda a: jnp.take_along_axis(a, blk_e, axis=1)
    local = jj - take(cum - nb_e)
    row0 = take(jnp.cumsum(n_pad, axis=1) - n_pad) * MOE_PAD + local * MOE_ROWS
    tail_half = (local == take(nb_e) - 1) & (take(n_pad) % per == 1)
    kind = jnp.where(j >= total, _BLK_SKIP, jnp.where(tail_half, _BLK_HALF, _BLK_FULL))
    return blk_e.reshape(-1), row0.reshape(-1), kind.reshape(-1).astype(jnp.int32)


def _moe_routed(h, x, mod, mod_spec2, route, cnt, wg, wu, wd, fg, *, tm):
    n, d = x.shape
    ne, _, ff = wg.shape
    n_tiles = n // tm
    n_blk = (2 * tm + ne * (MOE_ROWS - 1)) // MOE_ROWS
    blk_e, row0, kind = _moe_tables(cnt, n_blk)
    route_row = route[:, :8].reshape(n_tiles, tm, 8).transpose(0, 2, 1)
    wspec = lambda shape: pl.BlockSpec((None,) + shape, lambda i, j, be, r0, kd: (be[i * n_blk + j], 0, 0))
    once = lambda shape: pl.BlockSpec(shape, lambda i, j, be, r0, kd: (i, 0), pipeline_mode=pl.Buffered(1))
    return pl.pallas_call(
        functools.partial(_moe_routed_kernel, n_blk=n_blk),
        grid_spec=pltpu.PrefetchScalarGridSpec(
            num_scalar_prefetch=3,
            grid=(n_tiles, n_blk),
            in_specs=[
                once((tm, d)), once((tm, d)), mod_spec2,
                pl.BlockSpec((tm, LANES), lambda i, j, be, r0, kd: (i, 0)),
                pl.BlockSpec((None, 8, tm), lambda i, j, be, r0, kd: (i, 0, 0)),
                wspec((d, ff)), wspec((d, ff)), wspec((ff, d)),
                pl.BlockSpec((1, d), lambda i, j, be, r0, kd: (0, 0)),
            ],
            out_specs=pl.BlockSpec((tm, d), lambda i, j, be, r0, kd: (i, 0)),
            scratch_shapes=[pltpu.VMEM((tm, d), F32)],
        ),
        out_shape=jax.ShapeDtypeStruct((n, d), F32),
        compiler_params=_cparams("parallel", "arbitrary"),
        name="moe_routed",
    )(blk_e, row0, kind, h, x, mod, route, route_row, wg, wu, wd, fg)


def _rope_tables(pos):
    half = QK_ROPE // 2
    inv = jnp.exp(-math.log(ROPE_BASE) * jnp.arange(half, dtype=F32) / half)
    ang = pos.astype(F32)[:, None] * inv[None, :]
    cos, sin = jnp.cos(ang), jnp.sin(ang)
    reps = LANES // QK_ROPE
    tc = jnp.tile(jnp.concatenate([cos, cos], axis=-1), (1, reps))
    ts = jnp.tile(jnp.concatenate([-sin, sin], axis=-1), (1, reps))
    return tc, ts


def _swap_halves(w):
    half = w.shape[-1] // 2
    return jnp.concatenate([w[..., half:], w[..., :half]], axis=-1)


def kernel(x_prompt, x_sample, c_prompt, c_sample, state_conv, cache_ckv, cache_kpe, norm1_g, norm2_g, w_ada, b_ada, w_in_ab, b_in_ab, conv_w, conv_b, ln_conv_g, ln_conv_b, ln_v_g, ln_v_b, w_spatial, b_spatial, w_out_ab, b_out_ab, w_ffn_gu, w_ffn_down, w_dc, g_q, g_kv, w_uq, w_uk, w_uv, w_o, w_router, b_router, w_exp_gu, w_exp_down, final_g):
    nb, seq, d = x_prompt.shape
    ns_b, t_s, _ = x_sample.shape
    past = cache_ckv.shape[2]
    c_ch = conv_w.shape[-1]
    n_p = nb * seq
    n_s = ns_b * t_s
    tm = 512
    tq = 1024
    tmoe = 1024
    tiles_per_seq = seq // tm
    s_group = GMLP_CHUNK // t_s

    xp = x_prompt.reshape(n_p, d)
    xs = x_sample.reshape(n_s, d)

    mods = _adaln(jnp.concatenate([c_prompt, c_sample], axis=0), w_ada, b_ada)
    depth = w_ada.shape[0]
    mods = mods.reshape(depth, nb + ns_b, 6, d)

    def prompt_mod(l, k0):
        return mods[l, :nb, k0:k0 + 3][:, :, None, :]

    def sample_mod(l, k0):
        m = mods[l, nb:, k0:k0 + 3]
        m = jnp.broadcast_to(m[:, None], (ns_b, t_s, 3, d)).reshape(n_s, 3, d)
        return m.transpose(1, 0, 2)

    def pspec(tiles_per_b):
        return pl.BlockSpec((None, 3, 1, d), lambda i: (i // tiles_per_b, 0, 0, 0))

    def sspec(rows):
        return pl.BlockSpec((3, rows, d), lambda i: (0, i, 0))

    g1 = norm1_g[0][None]
    g2 = norm2_g[0][None]
    w_in = w_in_ab[0].astype(BF16)
    woa = w_out_ab[0][:c_ch].astype(BF16)
    wob = w_out_ab[0][c_ch:].astype(BF16)
    hd = c_ch // GMLP_HEADS
    bsp_p = jnp.repeat(b_spatial[0].T, hd, axis=1)
    wsp_p = w_spatial[0]
    eye = jnp.eye(s_group, dtype=F32)
    wsp_s = jnp.einsum('ab,hts->hatbs', eye, w_spatial[0][:, :t_s, :t_s]).reshape(
        GMLP_HEADS, GMLP_CHUNK, GMLP_CHUNK)
    bsp_s = jnp.tile(jnp.repeat(b_spatial[0][:, :t_s].T, hd, axis=1), (s_group, 1))
    common = (w_in, b_in_ab[0][None], conv_w[0], conv_b[0][None], ln_conv_g[0][None],
              ln_conv_b[0][None], ln_v_g[0][None], ln_v_b[0][None])
    tail = (woa, wob, b_out_ab[0][None])

    pm = prompt_mod(0, 0)
    xp, conv_p = _mixer(
        xp, pm, pl.BlockSpec((None, 3, 1, d), lambda o, i: (o, 0, 0, 0)), g1,
        common + (wsp_p, bsp_p) + tail, ns=1, t=tm, n_outer=nb, n_inner=tiles_per_seq)
    sm = sample_mod(0, 0)
    n_st = n_s // GMLP_CHUNK
    xs, conv_s, gv_s = _mixer(
        xs, sm, pl.BlockSpec((3, GMLP_CHUNK, d), lambda o, i: (0, o * n_st + i, 0)), g1,
        common + (wsp_s, bsp_s) + tail, ns=s_group, t=t_s, n_outer=1, n_inner=n_st,
        hist=state_conv[0])

    ff = w_ffn_gu.shape[-1] // 2
    wg = w_ffn_gu[0][:, :ff].astype(BF16)
    wu = w_ffn_gu[0][:, ff:].astype(BF16)
    wd = w_ffn_down[0].astype(BF16)
    xp = _ffn(xp, prompt_mod(0, 3), pspec(tiles_per_seq), g2, wg, wu, wd, tm=tm)
    xs = _ffn(xs, sample_mod(0, 3), sspec(n_s), g2, wg, wu, wd, tm=n_s)

    g1 = norm1_g[1][None]
    g2 = norm2_g[1][None]
    scale = (QK_NOPE + QK_ROPE) ** -0.5 * math.log2(math.e)
    zeros64 = jnp.zeros((d, QK_ROPE), F32)
    w_kpe = w_dc[0][:, Q_LORA + KV_LORA:]
    wdc = jnp.concatenate([w_dc[0][:, :Q_LORA + KV_LORA], w_kpe, zeros64,
                           _swap_halves(w_kpe), zeros64], axis=1).astype(BF16)
    wq3 = w_uq[0].reshape(Q_LORA, MLA_HEADS, QK_NOPE + QK_ROPE) * scale
    wq_nope = wq3[:, :, :QK_NOPE]
    wq_pe = wq3[:, :, QK_NOPE:]
    wq_sw = _swap_halves(wq_pe)
    zpad = jnp.zeros((Q_LORA, MLA_HEADS, QK_ROPE), F32)
    wqa = jnp.concatenate([wq_nope, wq_pe, zpad], axis=-1).reshape(Q_LORA, -1).astype(BF16)
    wqb = jnp.concatenate([wq_sw, zpad], axis=-1).reshape(Q_LORA, -1).astype(BF16)
    wuk = w_uk[0].astype(BF16)
    wuvt = w_uv[0].T.astype(BF16)
    gq = g_q[0][None]
    gkv = g_kv[0][None]
    wo = w_o[0].astype(BF16)
    ne = w_router.shape[-1]
    eff = w_exp_gu.shape[-1] // 2
    wr = jnp.zeros((d, LANES), F32).at[:, :ne].set(w_router[0])
    wrh = wr.astype(BF16)
    wrl = (wr - wrh.astype(F32)).astype(BF16)
    br = jnp.full((1, LANES), NEG_BIG, F32).at[0, :ne].set(b_router[0])
    weg = w_exp_gu[0][:, :, :eff].astype(BF16)
    weu = w_exp_gu[0][:, :, eff:].astype(BF16)
    wed = w_exp_down[0].astype(BF16)
    fg = final_g[None]

    tc_p, ts_p = _rope_tables(jnp.arange(seq))
    pm = prompt_mod(1, 0)
    ckv_p, kpe_p, q_p, k_p, vt_p = _mla_prompt(
        xp, pm, pspec(tiles_per_seq), g1, (wdc, gq, gkv, wqa, wqb, wuk, wuvt),
        tc_p, ts_p, tm=tm, tiles_per_seq=tiles_per_seq)
    o_p = _attn_prompt(q_p.reshape(nb, seq, -1), k_p.reshape(nb, seq, -1), vt_p, tq=tq)
    pm6 = mods[1, :nb][:, :, None, :]
    tiles_moe = seq // tmoe
    xp, hp, route_p, cnt_p = _oproj_route(
        xp, pm6, pl.BlockSpec((None, 6, 1, d), lambda i: (i // tiles_moe, 0, 0, 0)),
        o_p.reshape(n_p, -1), wo, g2, wrh, wrl, br, tm=tmoe)
    yp = _moe_routed(
        hp, xp, pm6, pl.BlockSpec((None, 6, 1, d), lambda i, j, *_: (i // tiles_moe, 0, 0, 0)),
        route_p, cnt_p, weg, weu, wed, fg, tm=tmoe)

    tc_s, ts_s = _rope_tables(past + jnp.arange(t_s))
    tc_s = jnp.tile(tc_s, (ns_b, 1))
    ts_s = jnp.tile(ts_s, (ns_b, 1))
    wqn = wq_nope.reshape(Q_LORA, -1).astype(BF16)
    wqp = wq_pe.reshape(Q_LORA, -1).astype(BF16)
    wqs = wq_sw.reshape(Q_LORA, -1).astype(BF16)
    wukt = w_uk[0].reshape(KV_LORA, MLA_HEADS, QK_NOPE).transpose(1, 2, 0).astype(BF16)
    wuv_h = w_uv[0].reshape(KV_LORA, MLA_HEADS, V_HD).transpose(1, 0, 2).astype(BF16)
    sm = sample_mod(1, 0)
    ckv_s, kpe_s, qlat, qpe = _mla_sample(
        xs, sm, sspec(n_s), g1, (wdc, gq, gkv, wqn, wqp, wqs, wukt), tc_s, ts_s, tm=n_s)
    o_lat = _attn_sample(
        qlat.reshape(ns_b, t_s * MLA_HEADS, KV_LORA), qpe.reshape(ns_b, t_s * MLA_HEADS, QK_ROPE),
        cache_ckv[0], cache_kpe[0], ckv_s.reshape(ns_b, t_s, KV_LORA),
        kpe_s.reshape(ns_b, t_s, QK_ROPE))
    sm6 = jnp.broadcast_to(mods[1, nb:][:, None], (ns_b, t_s, 6, d)).reshape(n_s, 6, d).transpose(1, 0, 2)
    xs, hs, route_s, cnt_s = _oproj_route(
        xs, sm6, pl.BlockSpec((6, n_s, d), lambda i: (0, i, 0)),
        o_lat.reshape(n_s, MLA_HEADS * KV_LORA), wo, g2, wrh, wrl, br, wuv_h, tm=n_s)
    ys = _moe_routed(
        hs, xs, sm6, pl.BlockSpec((6, n_s, d), lambda i, j, *_: (0, i, 0)),
        route_s, cnt_s, weg, weu, wed, fg, tm=n_s)

    return (yp.reshape(nb, seq, d), ys.reshape(ns_b, t_s, d),
            conv_p[None], conv_s[None], gv_s.reshape(ns_b, t_s, c_ch)[None],
            ckv_p.reshape(nb, seq, KV_LORA)[None], kpe_p.reshape(nb, seq, QK_ROPE)[None],
            ckv_s.reshape(ns_b, t_s, KV_LORA)[None], kpe_s.reshape(ns_b, t_s, QK_ROPE)[None])
```

```python
import functools
import math

import jax
import jax.numpy as jnp
from jax import lax
from jax.experimental import pallas as pl
from jax.experimental.pallas import tpu as pltpu

F32 = jnp.float32
BF16 = jnp.bfloat16

EPS = 1e-6
CHUNK = 64
CONV_W = 31
HIST = CONV_W - 1
HIST_PAD = 32
GMLP_CHUNK = 128
GMLP_HEADS = 8
MLA_HEADS = 8
Q_LORA = 256
KV_LORA = 256
QK_NOPE = 128
QK_ROPE = 64
V_HD = 128
QK_CAT = 256
ROPE_BASE = 10000.0
N_EXPERTS = 8
MOE_ROWS = 256
MOE_PAD = 128
LANES = 128
SUBLANES = 8
CONV_UNROLL = True
NEG_BIG = -1e30

VMEM_LIMIT = 56 * 1024 * 1024


def _cparams(*sem):
    return pltpu.CompilerParams(dimension_semantics=sem, vmem_limit_bytes=VMEM_LIMIT)


def _const_spec(shape):
    nd = len(shape)
    return pl.BlockSpec(shape, lambda *_: (0,) * nd, pipeline_mode=pl.Buffered(1))


def _modulate(x, g, shift, scale):
    y = x * lax.rsqrt(jnp.mean(x * x, axis=-1, keepdims=True) + EPS)
    return (y * g) * (1.0 + scale) + shift


def _layer_norm(x, g, b):
    mu = jnp.mean(x, axis=-1, keepdims=True)
    xc = x - mu
    var = jnp.mean(xc * xc, axis=-1, keepdims=True)
    return xc * lax.rsqrt(var + EPS) * g + b


def _dot(a, b):
    return jnp.dot(a, b, preferred_element_type=F32)


_NT = (((1,), (1,)), ((), ()))


def _adaln_kernel(c_ref, w_ref, b_ref, o_ref):
    c = c_ref[...]
    a = jax.nn.silu(c).astype(BF16)
    o_ref[...] = _dot(a, w_ref[...].astype(BF16)) + b_ref[...]


def _adaln(c_all, w_ada, b_ada, tn=1536):
    depth, d, n = w_ada.shape
    rows = c_all.shape[0]
    return pl.pallas_call(
        _adaln_kernel,
        grid=(depth, n // tn),
        in_specs=[
            pl.BlockSpec((rows, d), lambda l, j: (0, 0)),
            pl.BlockSpec((None, d, tn), lambda l, j: (l, 0, j)),
            pl.BlockSpec((None, 1, tn), lambda l, j: (l, 0, j)),
        ],
        out_specs=pl.BlockSpec((None, rows, tn), lambda l, j: (l, 0, j)),
        out_shape=jax.ShapeDtypeStruct((depth, rows, n), F32),
        compiler_params=_cparams("parallel", "parallel"),
        name="adaln",
    )(c_all, w_ada, b_ada.reshape(depth, 1, n))


def _mixer_kernel(*refs, ns, t, carry_hist, conv_rows):
    if carry_hist:
        (x_ref, mod_ref, g_ref, win_ref, bin_ref, cw_ref, cb_ref, lag_ref, lab_ref,
         lvg_ref, lvb_ref, wsp_ref, bsp_ref, woa_ref, wob_ref, bout_ref,
         xo_ref, conv_ref, aext_ref, ya_ref, yb_ref, tapwin_ref) = refs
        hist_ref = v_ref = None
    else:
        (x_ref, mod_ref, g_ref, win_ref, bin_ref, cw_ref, cb_ref, lag_ref, lab_ref,
         lvg_ref, lvb_ref, wsp_ref, bsp_ref, woa_ref, wob_ref, bout_ref, hist_ref,
         xo_ref, conv_ref, v_ref, aext_ref, ya_ref, yb_ref, tapwin_ref) = refs
    c = cw_ref.shape[1]
    rows = ns * t

    x = x_ref[...]
    h = _modulate(x, g_ref[...], mod_ref[0], mod_ref[1]).astype(BF16)
    p = _dot(h, win_ref[...]) + bin_ref[...]

    a = p[:, :c] * jax.nn.sigmoid(p[:, c:2 * c])
    if carry_hist:
        @pl.when(pl.program_id(1) == 0)
        def _():
            aext_ref[:, :HIST_PAD, :] = jnp.zeros((ns, HIST_PAD, c), F32)
    else:
        aext_ref[:, HIST_PAD - HIST:HIST_PAD, :] = hist_ref[...]
    aext_ref[:, HIST_PAD:, :] = a.reshape(ns, t, c)

    off = HIST_PAD - HIST
    n_cc = t // conv_rows
    lag = lag_ref[...]
    lab = lab_ref[...]
    cb = cb_ref[...]

    def finish(acc):
        y = _layer_norm(acc + cb, lag, lab)
        return jax.nn.silu(y).astype(BF16)

    def conv_direct(s):
        acc = jnp.zeros((conv_rows, c), F32)
        for k in range(CONV_W):
            acc = acc + aext_ref[s, off + k:off + k + conv_rows, :] * cw_ref[k:k + 1, :]
        return finish(acc)

    def conv_phased(s, r0):
        acc = jnp.zeros((conv_rows, c), F32)
        for k in range(CONV_W):
            r = (off + k) % SUBLANES
            q8 = pl.multiple_of(r0 + (off + k - r), SUBLANES)
            if r == 0:
                tap = aext_ref[s, pl.ds(q8, conv_rows), :]
            else:
                tap = tapwin_ref[r - 1, pl.ds(q8, conv_rows), :]
            acc = acc + tap * cw_ref[k:k + 1, :]
        return finish(acc)

    for s in range(ns):
        if n_cc == 1:
            ya_ref[s * t:(s + 1) * t, :] = conv_direct(s)
        else:
            span = t + HIST_PAD - SUBLANES
            for r in range(1, SUBLANES):
                tapwin_ref[r - 1] = aext_ref[s, r:r + span, :]

            def body(i, carry, s=s):
                r0 = pl.multiple_of(i * conv_rows, conv_rows)
                ya_ref[pl.ds(s * t + r0, conv_rows), :] = conv_phased(s, r0)
                return carry
            lax.fori_loop(0, n_cc, body, 0, unroll=CONV_UNROLL)

    conv_ref[...] = aext_ref[:, t + HIST_PAD - HIST:t + HIST_PAD, :]
    if carry_hist:
        aext_ref[:, :HIST_PAD, :] = aext_ref[:, t:t + HIST_PAD, :]

    z = jax.nn.gelu(p[:, 2 * c:])
    u = z[:, :c]
    v = _layer_norm(z[:, c:], lvg_ref[...], lvb_ref[...])
    if v_ref is not None:
        v_ref[...] = v
    ri = lax.broadcasted_iota(jnp.int32, (GMLP_CHUNK, GMLP_CHUNK), 0)
    ci = lax.broadcasted_iota(jnp.int32, (GMLP_CHUNK, GMLP_CHUNK), 1)
    tril = ri >= ci
    low_half = ci < (LANES // 2)
    w_heads = [jnp.where(tril, wsp_ref[hh], 0.0).astype(BF16) for hh in range(GMLP_HEADS)]
    for blk in range(rows // GMLP_CHUNK):
        r = slice(blk * GMLP_CHUNK, (blk + 1) * GMLP_CHUNK)
        for j in range(c // LANES):
            l = slice(j * LANES, (j + 1) * LANES)
            vp = v[r, l]
            lo = jnp.where(low_half, vp, 0.0).astype(BF16)
            hi = jnp.where(low_half, 0.0, vp).astype(BF16)
            sp = _dot(w_heads[2 * j], lo) + _dot(w_heads[2 * j + 1], hi)
            yb_ref[r, l] = (u[r, l] * (sp + bsp_ref[:, l])).astype(BF16)

    out = _dot(ya_ref[...], woa_ref[...]) + _dot(yb_ref[...], wob_ref[...]) + bout_ref[...]
    xo_ref[...] = x + mod_ref[2] * out


def _mixer(x, mod, mod_spec, g1, wts, *, ns, t, n_outer, n_inner, hist=None):
    (w_in, b_in, cw, cb, lag, lab, lvg, lvb, wsp, bsp, woa, wob, b_out) = wts
    d = x.shape[1]
    c = cw.shape[1]
    rows = ns * t
    carry_hist = hist is None
    tile = lambda o, i: (o * n_inner + i, 0)
    in_specs = [
        pl.BlockSpec((rows, d), tile),
        mod_spec,
        _const_spec((1, d)),
        _const_spec(w_in.shape), _const_spec(b_in.shape),
        _const_spec(cw.shape), _const_spec(cb.shape),
        _const_spec(lag.shape), _const_spec(lab.shape),
        _const_spec(lvg.shape), _const_spec(lvb.shape),
        _const_spec(wsp.shape), _const_spec(bsp.shape),
        _const_spec(woa.shape), _const_spec(wob.shape), _const_spec(b_out.shape),
    ]
    args = [x, mod, g1, w_in, b_in, cw, cb, lag, lab, lvg, lvb, wsp, bsp, woa, wob, b_out]
    n_tiles = n_outer * n_inner
    out_shape = [jax.ShapeDtypeStruct(x.shape, F32)]
    out_specs = [pl.BlockSpec((rows, d), tile)]
    if carry_hist:
        out_shape.append(jax.ShapeDtypeStruct((n_outer * ns, HIST, c), F32))
        out_specs.append(pl.BlockSpec((ns, HIST, c), lambda o, i: (o, 0, 0)))
    else:
        in_specs.append(pl.BlockSpec((ns, HIST, c), lambda o, i: (o * n_inner + i, 0, 0)))
        args.append(hist)
        out_shape.append(jax.ShapeDtypeStruct((n_tiles * ns, HIST, c), F32))
        out_specs.append(pl.BlockSpec((ns, HIST, c), lambda o, i: (o * n_inner + i, 0, 0)))
        out_shape.append(jax.ShapeDtypeStruct((x.shape[0], c), F32))
        out_specs.append(pl.BlockSpec((rows, c), tile))
    conv_rows = min(t, 64)
    return pl.pallas_call(
        functools.partial(_mixer_kernel, ns=ns, t=t, carry_hist=carry_hist, conv_rows=conv_rows),
        grid=(n_outer, n_inner),
        in_specs=in_specs,
        out_specs=out_specs,
        out_shape=out_shape,
        scratch_shapes=[
            pltpu.VMEM((ns, t + HIST_PAD, c), F32),
            pltpu.VMEM((rows, c), BF16),
            pltpu.VMEM((rows, c), BF16),
            pltpu.VMEM((SUBLANES - 1, t + HIST_PAD - SUBLANES, c), F32),
        ],
        compiler_params=_cparams("arbitrary", "arbitrary"),
        name="mixer_prompt" if carry_hist else "mixer_sample",
    )(*args)


def _ffn_kernel(x_ref, mod_ref, g_ref, wg_ref, wu_ref, wd_ref, o_ref, *, n_chunks):
    x = x_ref[...]
    h = _modulate(x, g_ref[...], mod_ref[0], mod_ref[1]).astype(BF16)
    ff = wg_ref.shape[1]
    cw = ff // n_chunks
    acc = None
    for j in range(n_chunks):
        sl = slice(j * cw, (j + 1) * cw)
        gate = _dot(h, wg_ref[:, sl])
        up = _dot(h, wu_ref[:, sl])
        act = (jax.nn.silu(gate) * up).astype(BF16)
        part = _dot(act, wd_ref[sl, :])
        acc = part if acc is None else acc + part
    o_ref[...] = x + mod_ref[2] * acc


def _ffn(x, mod, mod_spec, g2, wg, wu, wd, *, tm):
    n, d = x.shape
    return pl.pallas_call(
        functools.partial(_ffn_kernel, n_chunks=2),
        grid=(n // tm,),
        in_specs=[
            pl.BlockSpec((tm, d), lambda i: (i, 0)),
            mod_spec,
            _const_spec((1, d)),
            _const_spec(wg.shape), _const_spec(wu.shape), _const_spec(wd.shape),
        ],
        out_specs=pl.BlockSpec((tm, d), lambda i: (i, 0)),
        out_shape=jax.ShapeDtypeStruct((n, d), F32),
        compiler_params=_cparams("parallel"),
        name="ffn",
    )(x, mod, g2, wg, wu, wd)


def _mla_latents(x_ref, mod_ref, g_ref, wdc_ref, gq_ref, gkv_ref, tc, ts):
    x = x_ref[...]
    h = _modulate(x, g_ref[...], mod_ref[0], mod_ref[1]).astype(BF16)
    p = _dot(h, wdc_ref[...])
    cq = p[:, :Q_LORA]
    cq = cq * lax.rsqrt(jnp.mean(cq * cq, axis=-1, keepdims=True) + EPS) * gq_ref[...]
    ckv = p[:, Q_LORA:Q_LORA + KV_LORA]
    ckv = ckv * lax.rsqrt(jnp.mean(ckv * ckv, axis=-1, keepdims=True) + EPS) * gkv_ref[...]
    b0 = Q_LORA + KV_LORA
    kpe = p[:, b0:b0 + LANES] * tc + p[:, b0 + LANES:b0 + 2 * LANES] * ts
    return cq.astype(BF16), ckv, kpe


def _mla_prompt_kernel(x_ref, mod_ref, g_ref, wdc_ref, gq_ref, gkv_ref, tc_ref, ts_ref,
                       wqa_ref, wqb_ref, wuk_ref, wuvt_ref,
                       ckv_ref, kpe_ref, q_ref, k_ref, vt_ref):
    tc = tc_ref[...]
    ts = ts_ref[...]
    cq, ckv, kpe = _mla_latents(x_ref, mod_ref, g_ref, wdc_ref, gq_ref, gkv_ref, tc, ts)
    ckv_ref[...] = ckv
    kpe_ref[...] = kpe[:, :QK_ROPE]
    qa = _dot(cq, wqa_ref[...])
    qb = _dot(cq, wqb_ref[...])
    ckv_b = ckv.astype(BF16)
    kn = _dot(ckv_b, wuk_ref[...])
    vt_ref[...] = lax.dot_general(wuvt_ref[...], ckv_b, _NT, preferred_element_type=F32).astype(BF16)
    kpe_b = kpe.astype(BF16)
    for hh in range(MLA_HEADS):
        base = hh * QK_CAT
        q_ref[:, base:base + QK_NOPE] = qa[:, base:base + QK_NOPE].astype(BF16)
        q_ref[:, base + QK_NOPE:base + QK_CAT] = (
            qa[:, base + QK_NOPE:base + QK_CAT] * tc
            + qb[:, hh * LANES:(hh + 1) * LANES] * ts).astype(BF16)
        k_ref[:, base:base + QK_NOPE] = kn[:, hh * QK_NOPE:(hh + 1) * QK_NOPE].astype(BF16)
        k_ref[:, base + QK_NOPE:base + QK_CAT] = kpe_b


def _mla_prompt(x, mod, mod_spec, g1, wts, tc, ts, *, tm, tiles_per_seq):
    wdc, gq, gkv, wqa, wqb, wuk, wuvt = wts
    n, d = x.shape
    hq = MLA_HEADS * QK_CAT
    hv = MLA_HEADS * V_HD
    nb = n // (tm * tiles_per_seq)
    row = lambda i: (i, 0)
    pos = lambda i: (i % tiles_per_seq, 0)
    return pl.pallas_call(
        _mla_prompt_kernel,
        grid=(n // tm,),
        in_specs=[
            pl.BlockSpec((tm, d), row), mod_spec, _const_spec((1, d)),
            _const_spec(wdc.shape), _const_spec(gq.shape), _const_spec(gkv.shape),
            pl.BlockSpec((tm, LANES), pos), pl.BlockSpec((tm, LANES), pos),
            _const_spec(wqa.shape), _const_spec(wqb.shape),
            _const_spec(wuk.shape), _const_spec(wuvt.shape),
        ],
        out_specs=[
            pl.BlockSpec((tm, KV_LORA), row), pl.BlockSpec((tm, QK_ROPE), row),
            pl.BlockSpec((tm, hq), row), pl.BlockSpec((tm, hq), row),
            pl.BlockSpec((None, hv, tm), lambda i: (i // tiles_per_seq, 0, i % tiles_per_seq)),
        ],
        out_shape=[
            jax.ShapeDtypeStruct((n, KV_LORA), F32), jax.ShapeDtypeStruct((n, QK_ROPE), F32),
            jax.ShapeDtypeStruct((n, hq), BF16), jax.ShapeDtypeStruct((n, hq), BF16),
            jax.ShapeDtypeStruct((nb, hv, tm * tiles_per_seq), BF16),
        ],
        compiler_params=_cparams("parallel"),
        name="mla_proj_prompt",
    )(x, mod, g1, wdc, gq, gkv, tc, ts, wqa, wqb, wuk, wuvt)


def _mla_sample_kernel(x_ref, mod_ref, g_ref, wdc_ref, gq_ref, gkv_ref, tc_ref, ts_ref,
                       wqn_ref, wqp_ref, wqs_ref, wukt_ref,
                       ckv_ref, kpe_ref, qlat_ref, qpe_ref):
    tc = tc_ref[...]
    ts = ts_ref[...]
    cq, ckv, kpe = _mla_latents(x_ref, mod_ref, g_ref, wdc_ref, gq_ref, gkv_ref, tc, ts)
    ckv_ref[...] = ckv
    kpe_ref[...] = kpe[:, :QK_ROPE]
    qn = _dot(cq, wqn_ref[...]).astype(BF16)
    qp = _dot(cq, wqp_ref[...])
    qs = _dot(cq, wqs_ref[...])
    for j in range(MLA_HEADS * QK_ROPE // LANES):
        l = slice(j * LANES, (j + 1) * LANES)
        qpe_ref[:, l] = (qp[:, l] * tc + qs[:, l] * ts).astype(BF16)
    for hh in range(MLA_HEADS):
        qlat_ref[:, hh * KV_LORA:(hh + 1) * KV_LORA] = _dot(
            qn[:, hh * QK_NOPE:(hh + 1) * QK_NOPE], wukt_ref[hh]).astype(BF16)


def _mla_sample(x, mod, mod_spec, g1, wts, tc, ts, *, tm):
    wdc, gq, gkv, wqn, wqp, wqs, wukt = wts
    n, d = x.shape
    row = lambda i: (i, 0)
    return pl.pallas_call(
        _mla_sample_kernel,
        grid=(n // tm,),
        in_specs=[
            pl.BlockSpec((tm, d), row), mod_spec, _const_spec((1, d)),
            _const_spec(wdc.shape), _const_spec(gq.shape), _const_spec(gkv.shape),
            pl.BlockSpec((tm, LANES), row), pl.BlockSpec((tm, LANES), row),
            _const_spec(wqn.shape), _const_spec(wqp.shape), _const_spec(wqs.shape),
            _const_spec(wukt.shape),
        ],
        out_specs=[
            pl.BlockSpec((tm, KV_LORA), row), pl.BlockSpec((tm, QK_ROPE), row),
            pl.BlockSpec((tm, MLA_HEADS * KV_LORA), row),
            pl.BlockSpec((tm, MLA_HEADS * QK_ROPE), row),
        ],
        out_shape=[
            jax.ShapeDtypeStruct((n, KV_LORA), F32), jax.ShapeDtypeStruct((n, QK_ROPE), F32),
            jax.ShapeDtypeStruct((n, MLA_HEADS * KV_LORA), BF16),
            jax.ShapeDtypeStruct((n, MLA_HEADS * QK_ROPE), BF16),
        ],
        compiler_params=_cparams("parallel"),
        name="mla_proj_sample",
    )(x, mod, g1, wdc, gq, gkv, tc, ts, wqn, wqp, wqs, wukt)


def _attn_prompt_kernel(q_ref, k_ref, vt_ref, o_ref, s_scr, p_scr, acc_scr, *, tq, tk):
    qi = pl.program_id(2)
    n_full = 2 * qi

    def qk_to(slot, t):
        k = k_ref[pl.ds(pl.multiple_of(t * tk, tk), tk), :]
        s = lax.dot_general(k, q_ref[...], _NT, preferred_element_type=F32)
        s_scr[slot] = s
        return jnp.max(s, axis=0, keepdims=True)

    def softmax_from(slot, m, l, bm):
        m_new = jnp.maximum(m, bm)
        alpha = jnp.exp2(m - m_new)
        p = jnp.exp2(s_scr[slot] - m_new)
        p_scr[slot] = p.astype(BF16)
        l = alpha * l + jnp.sum(p, axis=0, keepdims=True)
        return m_new, l, alpha

    def pv_from(slot, t, alpha):
        vt = vt_ref[:, pl.ds(pl.multiple_of(t * tk, tk), tk)]
        acc_scr[...] = alpha * acc_scr[...] + _dot(vt, p_scr[slot])

    p_scr[1] = jnp.zeros((tk, tq), BF16)
    acc_scr[...] = jnp.zeros((V_HD, tq), F32)
    m0 = jnp.full((1, tq), NEG_BIG, F32)
    l0 = jnp.zeros((1, tq), F32)
    bm0 = qk_to(0, 0)

    def body(i, carry):
        m, l, bm, a_prev = carry
        t = 2 * i
        bm1 = qk_to(1, t + 1)
        m, l, a0 = softmax_from(0, m, l, bm)
        pv_from(1, jnp.maximum(t - 1, 0), a_prev)
        bm2 = qk_to(0, t + 2)
        m, l, a1 = softmax_from(1, m, l, bm1)
        pv_from(0, t, a0)
        return m, l, bm2, a1

    m, l, bm, a_prev = lax.fori_loop(0, qi, body, (m0, l0, bm0, jnp.ones((1, tq), F32)))
    pv_from(1, jnp.maximum(n_full - 1, 0), a_prev)
    kk = lax.broadcasted_iota(jnp.int32, (tk, tk), 0) // CHUNK
    qq = lax.broadcasted_iota(jnp.int32, (tk, tk), 1) // CHUNK
    vis = qq >= kk
    s_a = jnp.where(vis, s_scr[0, :, :tk], NEG_BIG)
    bm_d = jnp.concatenate([jnp.max(s_a, axis=0, keepdims=True), bm[:, tk:]], axis=1)
    m_new = jnp.maximum(m, bm_d)
    alpha = jnp.exp2(m - m_new)
    p_a = jnp.exp2(s_a - m_new[:, :tk])
    p_b = jnp.exp2(s_scr[0, :, tk:] - m_new[:, tk:])
    p_scr[0, :, :tk] = p_a.astype(BF16)
    p_scr[0, :, tk:] = p_b.astype(BF16)
    l = alpha * l + jnp.concatenate([jnp.sum(p_a, axis=0, keepdims=True),
                                     jnp.sum(p_b, axis=0, keepdims=True)], axis=1)
    pv_from(0, n_full, alpha)
    k1 = k_ref[pl.ds(pl.multiple_of((n_full + 1) * tk, tk), tk), :]
    s1 = lax.dot_general(k1, q_ref[tk:, :], _NT, preferred_element_type=F32)
    s1 = jnp.where(vis, s1, NEG_BIG)
    m_b = m_new[:, tk:]
    m_b2 = jnp.maximum(m_b, jnp.max(s1, axis=0, keepdims=True))
    a1 = jnp.exp2(m_b - m_b2)
    p1 = jnp.exp2(s1 - m_b2)
    l_b = a1 * l[:, tk:] + jnp.sum(p1, axis=0, keepdims=True)
    vt1 = vt_ref[:, pl.ds(pl.multiple_of((n_full + 1) * tk, tk), tk)]
    acc_b = a1 * acc_scr[:, tk:] + _dot(vt1, p1.astype(BF16))
    o_ref[:tk, :] = (acc_scr[:, :tk] / l[:, :tk]).T.astype(o_ref.dtype)
    o_ref[tk:, :] = (acc_b / l_b).T.astype(o_ref.dtype)


def _attn_prompt(q, k, vt, *, tq):
    b, s, _ = q.shape
    tk = tq // 2
    return pl.pallas_call(
        functools.partial(_attn_prompt_kernel, tq=tq, tk=tk),
        grid=(b, MLA_HEADS, s // tq),
        in_specs=[
            pl.BlockSpec((None, tq, QK_CAT), lambda bi, hi, qi: (bi, qi, hi)),
            pl.BlockSpec((None, s, QK_CAT), lambda bi, hi, qi: (bi, 0, hi)),
            pl.BlockSpec((None, V_HD, s), lambda bi, hi, qi: (bi, hi, 0)),
        ],
        out_specs=pl.BlockSpec((None, tq, V_HD), lambda bi, hi, qi: (bi, qi, hi)),
        out_shape=jax.ShapeDtypeStruct((b, s, MLA_HEADS * V_HD), BF16),
        scratch_shapes=[pltpu.VMEM((2, tk, tq), F32), pltpu.VMEM((2, tk, tq), BF16),
                        pltpu.VMEM((V_HD, tq), F32)],
        compiler_params=_cparams("parallel", "parallel", "arbitrary"),
        name="attn_prompt",
    )(q, k, vt)


def _attn_sample_kernel(ql_ref, qp_ref, cc_ref, cp_ref, nc_ref, np_ref, o_ref):
    ql = ql_ref[...]
    qp = qp_ref[...]
    nt = (((1,), (1,)), ((), ()))
    cc = cc_ref[...].astype(BF16)
    cp = cp_ref[...].astype(BF16)
    nc = nc_ref[...].astype(BF16)
    npe = np_ref[...].astype(BF16)
    s_c = (lax.dot_general(ql, cc, nt, preferred_element_type=F32)
           + lax.dot_general(qp, cp, nt, preferred_element_type=F32))
    s_n = (lax.dot_general(ql, nc, nt, preferred_element_type=F32)
           + lax.dot_general(qp, npe, nt, preferred_element_type=F32))
    m = jnp.maximum(jnp.max(s_c, axis=-1, keepdims=True), jnp.max(s_n, axis=-1, keepdims=True))
    p_c = jnp.exp2(s_c - m)
    p_n = jnp.exp2(s_n - m)
    l = jnp.sum(p_c, axis=-1, keepdims=True) + jnp.sum(p_n, axis=-1, keepdims=True)
    o = _dot(p_c.astype(BF16), cc) + _dot(p_n.astype(BF16), nc)
    o_ref[...] = (o / l).astype(o_ref.dtype)


def _attn_sample(qlat, qpe, cache_ckv, cache_kpe, ckv_new, kpe_new):
    nb, past, _ = cache_ckv.shape
    r = qlat.shape[1]
    t = ckv_new.shape[1]
    blk = lambda shape: pl.BlockSpec((None,) + shape, lambda i: (i, 0, 0))
    return pl.pallas_call(
        _attn_sample_kernel,
        grid=(nb,),
        in_specs=[blk((r, KV_LORA)), blk((r, QK_ROPE)), blk((past, KV_LORA)),
                  blk((past, QK_ROPE)), blk((t, KV_LORA)), blk((t, QK_ROPE))],
        out_specs=blk((r, KV_LORA)),
        out_shape=jax.ShapeDtypeStruct((nb, r, KV_LORA), BF16),
        compiler_params=_cparams("parallel"),
        name="attn_sample",
    )(qlat, qpe, cache_ckv, cache_kpe, ckv_new, kpe_new)


def _route_tail(xm, mod_ref, g_ref, wrh_ref, wrl_ref, br_ref, xo_ref, h_ref, route_ref, cnt_ref):
    tm = xm.shape[0]
    xo_ref[...] = xm
    hf = _modulate(xm, g_ref[...], mod_ref[3], mod_ref[4])
    hb = hf.astype(BF16)
    h_ref[...] = hb
    h_lo = (hf - hb.astype(F32)).astype(BF16)
    logits = (_dot(hb, wrh_ref[...]) + _dot(hb, wrl_ref[...])
              + _dot(h_lo, wrh_ref[...])) + br_ref[...]
    lane = lax.broadcasted_iota(jnp.int32, (tm, LANES), 1)
    m1 = jnp.max(logits, axis=-1, keepdims=True)
    i1 = jnp.min(jnp.where(logits == m1, lane, LANES), axis=-1, keepdims=True)
    rest = jnp.where(lane == i1, -jnp.inf, logits)
    m2 = jnp.max(rest, axis=-1, keepdims=True)
    i2 = jnp.min(jnp.where(rest == m2, lane, LANES), axis=-1, keepdims=True)
    e2 = jnp.exp(m2 - m1)
    den = 1.0 + e2
    sel1 = lane == i1
    sel2 = lane == i2
    onehot = jnp.where(sel1, 1.0, jnp.where(sel2, 1.0, 0.0))
    ri = lax.broadcasted_iota(jnp.int32, (tm, tm), 0)
    ci = lax.broadcasted_iota(jnp.int32, (tm, tm), 1)
    before = jnp.where(ri > ci, 1.0, 0.0).astype(BF16)
    rank = _dot(before, onehot.astype(BF16))
    cnt = jnp.sum(onehot, axis=0, keepdims=True)
    nblk = jnp.floor((cnt + (MOE_PAD - 1)) * (1.0 / MOE_PAD))
    r8 = lax.broadcasted_iota(jnp.int32, (LANES, LANES), 0)
    c8 = lax.broadcasted_iota(jnp.int32, (LANES, LANES), 1)
    upper = jnp.where(r8 < c8, 1.0, 0.0).astype(BF16)
    first_blk = _dot(jnp.broadcast_to(nblk, (8, LANES)).astype(BF16), upper)[0:1]
    slot = first_blk * MOE_PAD + rank
    d1 = jnp.sum(jnp.where(sel1, slot, 0.0), axis=-1, keepdims=True)
    d2 = jnp.sum(jnp.where(sel2, slot, 0.0), axis=-1, keepdims=True)
    route_ref[...] = jnp.where(lane == 0, d1, jnp.where(lane == 1, d2, jnp.where(
        lane == 2, 1.0 / den, jnp.where(lane == 3, e2 / den, 0.0))))
    cnt_ref[...] = jnp.broadcast_to(cnt, (8, LANES))


def _oproj_route_kernel(x_ref, mod_ref, o_ref, wo_ref, g_ref, wrh_ref, wrl_ref, br_ref,
                        xo_ref, h_ref, route_ref, cnt_ref):
    xm = x_ref[...] + mod_ref[2] * _dot(o_ref[...], wo_ref[...])
    _route_tail(xm, mod_ref, g_ref, wrh_ref, wrl_ref, br_ref, xo_ref, h_ref, route_ref, cnt_ref)


def _oproj_latent_route_kernel(x_ref, mod_ref, o_ref, wuv_ref, wo_ref, g_ref, wrh_ref, wrl_ref,
                               br_ref, xo_ref, h_ref, route_ref, cnt_ref):
    acc = None
    for hh in range(MLA_HEADS):
        oh = _dot(o_ref[:, hh * KV_LORA:(hh + 1) * KV_LORA], wuv_ref[hh]).astype(BF16)
        part = _dot(oh, wo_ref[hh * V_HD:(hh + 1) * V_HD, :])
        acc = part if acc is None else acc + part
    xm = x_ref[...] + mod_ref[2] * acc
    _route_tail(xm, mod_ref, g_ref, wrh_ref, wrl_ref, br_ref, xo_ref, h_ref, route_ref, cnt_ref)


def _oproj_route(x, mod, mod_spec, o, wo, g2, wrh, wrl, br, wuv=None, *, tm):
    n, d = x.shape
    row = lambda i: (i, 0)
    in_specs = [pl.BlockSpec((tm, d), row), mod_spec, pl.BlockSpec((tm, o.shape[1]), row)]
    args = [x, mod, o]
    if wuv is not None:
        in_specs.append(_const_spec(wuv.shape))
        args.append(wuv)
    in_specs += [_const_spec(wo.shape), _const_spec((1, d)), _const_spec(wrh.shape),
                 _const_spec(wrl.shape), _const_spec(br.shape)]
    args += [wo, g2, wrh, wrl, br]
    return pl.pallas_call(
        _oproj_route_kernel if wuv is None else _oproj_latent_route_kernel,
        grid=(n // tm,),
        in_specs=in_specs,
        out_specs=[pl.BlockSpec((tm, d), row), pl.BlockSpec((tm, d), row),
                   pl.BlockSpec((tm, LANES), row), pl.BlockSpec((None, 8, LANES), lambda i: (i, 0, 0))],
        out_shape=[jax.ShapeDtypeStruct((n, d), F32), jax.ShapeDtypeStruct((n, d), BF16),
                   jax.ShapeDtypeStruct((n, LANES), F32),
                   jax.ShapeDtypeStruct((n // tm, 8, LANES), F32)],
        compiler_params=_cparams("parallel"),
        name="oproj_route" if wuv is None else "oproj_latent_route",
    )(*args)


_BLK_SKIP, _BLK_FULL, _BLK_HALF = 0, 1, 2


def _moe_routed_kernel(be_ref, row0_ref, kind_ref, h_ref, x_ref, mod_ref, rc_ref, rr_ref,
                       wg_ref, wu_ref, wd_ref, fg_ref, y_ref, acc_ref, *, n_blk):
    i = pl.program_id(0)
    j = pl.program_id(1)
    tm = h_ref.shape[0]
    kind = kind_ref[i * n_blk + j]

    @pl.when(j == 0)
    def _():
        acc_ref[...] = jnp.zeros(acc_ref.shape, F32)

    def block(rows):
        base = row0_ref[i * n_blk + j].astype(F32)
        rr = rr_ref[...]
        srow = lax.broadcasted_iota(jnp.int32, (rows, tm), 0).astype(F32) + base
        hit1 = rr[0:1, :] == srow
        hit2 = rr[1:2, :] == srow
        sel = jnp.where(hit1, 1.0, jnp.where(hit2, 1.0, 0.0)).astype(BF16)
        gate = jnp.sum(jnp.where(hit1, rr[2:3, :], jnp.where(hit2, rr[3:4, :], 0.0)),
                       axis=-1, keepdims=True)
        hb = _dot(sel, h_ref[...]).astype(BF16)
        act = (jax.nn.silu(_dot(hb, wg_ref[...])) * _dot(hb, wu_ref[...])).astype(BF16)
        yb = (_dot(act, wd_ref[...]) * gate).astype(BF16)
        rc = rc_ref[...]
        scol = lax.broadcasted_iota(jnp.int32, (tm, rows), 1).astype(F32) + base
        sel_t = jnp.where(rc[:, 0:1] == scol, 1.0,
                          jnp.where(rc[:, 1:2] == scol, 1.0, 0.0)).astype(BF16)
        acc_ref[...] += _dot(sel_t, yb)

    @pl.when(kind == _BLK_FULL)
    def _():
        block(MOE_ROWS)

    @pl.when(kind == _BLK_HALF)
    def _():
        block(MOE_PAD)

    @pl.when(j == pl.num_programs(1) - 1)
    def _():
        xn = x_ref[...] + mod_ref[5] * acc_ref[...]
        y_ref[...] = xn * lax.rsqrt(jnp.mean(xn * xn, axis=-1, keepdims=True) + EPS) * fg_ref[...]


def _moe_tables(cnt, n_blk):
    c = cnt[:, 0, :N_EXPERTS]
    n_pad = jnp.ceil(c / MOE_PAD).astype(jnp.int32)
    per = MOE_ROWS // MOE_PAD
    nb_e = (n_pad + per - 1) // per
    cum = jnp.cumsum(nb_e, axis=1)
    total = cum[:, -1:]
    j = jnp.arange(n_blk, dtype=jnp.int32)[None, :]
    jj = jnp.minimum(j, total - 1)
    blk_e = jnp.sum((cum[:, None, :] <= jj[:, :, None]).astype(jnp.int32), axis=-1)
    take = lambda a: jnp.take_along_axis(a, blk_e, axis=1)
    local = jj - take(cum - nb_e)
    odd = take(n_pad) % per == 1
    head_half = (local == 0) & odd
    row0 = (take(jnp.cumsum(n_pad, axis=1) - n_pad) * MOE_PAD + local * MOE_ROWS
            - jnp.where(odd & (local > 0), MOE_ROWS - MOE_PAD, 0))
    kind = jnp.where(j >= total, _BLK_SKIP, jnp.where(head_half, _BLK_HALF, _BLK_FULL))
    return blk_e.reshape(-1), row0.reshape(-1), kind.reshape(-1).astype(jnp.int32)


def _moe_routed(h, x, mod, mod_spec2, route, cnt, wg, wu, wd, fg, *, tm):
    n, d = x.shape
    ne, _, ff = wg.shape
    n_tiles = n // tm
    n_blk = (2 * tm + ne * (MOE_ROWS - 1)) // MOE_ROWS
    blk_e, row0, kind = _moe_tables(cnt, n_blk)
    route_row = route[:, :8].reshape(n_tiles, tm, 8).transpose(0, 2, 1)
    wspec = lambda shape: pl.BlockSpec((None,) + shape, lambda i, j, be, r0, kd: (be[i * n_blk + j], 0, 0))
    once = lambda shape: pl.BlockSpec(shape, lambda i, j, be, r0, kd: (i, 0), pipeline_mode=pl.Buffered(1))
    return pl.pallas_call(
        functools.partial(_moe_routed_kernel, n_blk=n_blk),
        grid_spec=pltpu.PrefetchScalarGridSpec(
            num_scalar_prefetch=3,
            grid=(n_tiles, n_blk),
            in_specs=[
                pl.BlockSpec((tm, d), lambda i, j, be, r0, kd: (i, 0)), once((tm, d)), mod_spec2,
                pl.BlockSpec((tm, LANES), lambda i, j, be, r0, kd: (i, 0)),
                pl.BlockSpec((None, 8, tm), lambda i, j, be, r0, kd: (i, 0, 0)),
                wspec((d, ff)), wspec((d, ff)), wspec((ff, d)),
                pl.BlockSpec((1, d), lambda i, j, be, r0, kd: (0, 0)),
            ],
            out_specs=pl.BlockSpec((tm, d), lambda i, j, be, r0, kd: (i, 0)),
            scratch_shapes=[pltpu.VMEM((tm, d), F32)],
        ),
        out_shape=jax.ShapeDtypeStruct((n, d), F32),
        compiler_params=_cparams("parallel", "arbitrary"),
        name="moe_routed",
    )(blk_e, row0, kind, h, x, mod, route, route_row, wg, wu, wd, fg)


def _rope_tables(pos):
    half = QK_ROPE // 2
    inv = jnp.exp(-math.log(ROPE_BASE) * jnp.arange(half, dtype=F32) / half)
    ang = pos.astype(F32)[:, None] * inv[None, :]
    cos, sin = jnp.cos(ang), jnp.sin(ang)
    reps = LANES // QK_ROPE
    tc = jnp.tile(jnp.concatenate([cos, cos], axis=-1), (1, reps))
    ts = jnp.tile(jnp.concatenate([-sin, sin], axis=-1), (1, reps))
    return tc, ts


def _swap_halves(w):
    half = w.shape[-1] // 2
    return jnp.concatenate([w[..., half:], w[..., :half]], axis=-1)


def kernel(x_prompt, x_sample, c_prompt, c_sample, state_conv, cache_ckv, cache_kpe, norm1_g, norm2_g, w_ada, b_ada, w_in_ab, b_in_ab, conv_w, conv_b, ln_conv_g, ln_conv_b, ln_v_g, ln_v_b, w_spatial, b_spatial, w_out_ab, b_out_ab, w_ffn_gu, w_ffn_down, w_dc, g_q, g_kv, w_uq, w_uk, w_uv, w_o, w_router, b_router, w_exp_gu, w_exp_down, final_g):
    nb, seq, d = x_prompt.shape
    ns_b, t_s, _ = x_sample.shape
    past = cache_ckv.shape[2]
    c_ch = conv_w.shape[-1]
    n_p = nb * seq
    n_s = ns_b * t_s
    tm = 512
    tq = 1024
    tmoe = 1024
    tiles_per_seq = seq // tm
    s_group = GMLP_CHUNK // t_s

    xp = x_prompt.reshape(n_p, d)
    xs = x_sample.reshape(n_s, d)

    mods = _adaln(jnp.concatenate([c_prompt, c_sample], axis=0), w_ada, b_ada)
    depth = w_ada.shape[0]
    mods = mods.reshape(depth, nb + ns_b, 6, d)

    def prompt_mod(l, k0):
        return mods[l, :nb, k0:k0 + 3][:, :, None, :]

    def sample_mod(l, k0):
        m = mods[l, nb:, k0:k0 + 3]
        m = jnp.broadcast_to(m[:, None], (ns_b, t_s, 3, d)).reshape(n_s, 3, d)
        return m.transpose(1, 0, 2)

    def pspec(tiles_per_b):
        return pl.BlockSpec((None, 3, 1, d), lambda i: (i // tiles_per_b, 0, 0, 0))

    def sspec(rows):
        return pl.BlockSpec((3, rows, d), lambda i: (0, i, 0))

    g1 = norm1_g[0][None]
    g2 = norm2_g[0][None]
    w_in = w_in_ab[0].astype(BF16)
    woa = w_out_ab[0][:c_ch].astype(BF16)
    wob = w_out_ab[0][c_ch:].astype(BF16)
    hd = c_ch // GMLP_HEADS
    bsp_p = jnp.repeat(b_spatial[0].T, hd, axis=1)
    wsp_p = w_spatial[0]
    eye = jnp.eye(s_group, dtype=F32)
    wsp_s = jnp.einsum('ab,hts->hatbs', eye, w_spatial[0][:, :t_s, :t_s]).reshape(
        GMLP_HEADS, GMLP_CHUNK, GMLP_CHUNK)
    bsp_s = jnp.tile(jnp.repeat(b_spatial[0][:, :t_s].T, hd, axis=1), (s_group, 1))
    common = (w_in, b_in_ab[0][None], conv_w[0], conv_b[0][None], ln_conv_g[0][None],
              ln_conv_b[0][None], ln_v_g[0][None], ln_v_b[0][None])
    tail = (woa, wob, b_out_ab[0][None])

    pm = prompt_mod(0, 0)
    xp, conv_p = _mixer(
        xp, pm, pl.BlockSpec((None, 3, 1, d), lambda o, i: (o, 0, 0, 0)), g1,
        common + (wsp_p, bsp_p) + tail, ns=1, t=tm, n_outer=nb, n_inner=tiles_per_seq)
    sm = sample_mod(0, 0)
    n_st = n_s // GMLP_CHUNK
    xs, conv_s, gv_s = _mixer(
        xs, sm, pl.BlockSpec((3, GMLP_CHUNK, d), lambda o, i: (0, o * n_st + i, 0)), g1,
        common + (wsp_s, bsp_s) + tail, ns=s_group, t=t_s, n_outer=1, n_inner=n_st,
        hist=state_conv[0])

    ff = w_ffn_gu.shape[-1] // 2
    wg = w_ffn_gu[0][:, :ff].astype(BF16)
    wu = w_ffn_gu[0][:, ff:].astype(BF16)
    wd = w_ffn_down[0].astype(BF16)
    xp = _ffn(xp, prompt_mod(0, 3), pspec(tiles_per_seq), g2, wg, wu, wd, tm=tm)
    xs = _ffn(xs, sample_mod(0, 3), sspec(n_s), g2, wg, wu, wd, tm=n_s)

    g1 = norm1_g[1][None]
    g2 = norm2_g[1][None]
    scale = (QK_NOPE + QK_ROPE) ** -0.5 * math.log2(math.e)
    zeros64 = jnp.zeros((d, QK_ROPE), F32)
    w_kpe = w_dc[0][:, Q_LORA + KV_LORA:]
    wdc = jnp.concatenate([w_dc[0][:, :Q_LORA + KV_LORA], w_kpe, zeros64,
                           _swap_halves(w_kpe), zeros64], axis=1).astype(BF16)
    wq3 = w_uq[0].reshape(Q_LORA, MLA_HEADS, QK_NOPE + QK_ROPE) * scale
    wq_nope = wq3[:, :, :QK_NOPE]
    wq_pe = wq3[:, :, QK_NOPE:]
    wq_sw = _swap_halves(wq_pe)
    zpad = jnp.zeros((Q_LORA, MLA_HEADS, QK_ROPE), F32)
    wqa = jnp.concatenate([wq_nope, wq_pe, zpad], axis=-1).reshape(Q_LORA, -1).astype(BF16)
    wqb = jnp.concatenate([wq_sw, zpad], axis=-1).reshape(Q_LORA, -1).astype(BF16)
    wuk = w_uk[0].astype(BF16)
    wuvt = w_uv[0].T.astype(BF16)
    gq = g_q[0][None]
    gkv = g_kv[0][None]
    wo = w_o[0].astype(BF16)
    ne = w_router.shape[-1]
    eff = w_exp_gu.shape[-1] // 2
    wr = jnp.zeros((d, LANES), F32).at[:, :ne].set(w_router[0])
    wrh = wr.astype(BF16)
    wrl = (wr - wrh.astype(F32)).astype(BF16)
    br = jnp.full((1, LANES), NEG_BIG, F32).at[0, :ne].set(b_router[0])
    weg = w_exp_gu[0][:, :, :eff].astype(BF16)
    weu = w_exp_gu[0][:, :, eff:].astype(BF16)
    wed = w_exp_down[0].astype(BF16)
    fg = final_g[None]

    tc_p, ts_p = _rope_tables(jnp.arange(seq))
    pm = prompt_mod(1, 0)
    ckv_p, kpe_p, q_p, k_p, vt_p = _mla_prompt(
        xp, pm, pspec(tiles_per_seq), g1, (wdc, gq, gkv, wqa, wqb, wuk, wuvt),
        tc_p, ts_p, tm=tm, tiles_per_seq=tiles_per_seq)
    o_p = _attn_prompt(q_p.reshape(nb, seq, -1), k_p.reshape(nb, seq, -1), vt_p, tq=tq)
    pm6 = mods[1, :nb][:, :, None, :]
    tiles_moe = seq // tmoe
    xp, hp, route_p, cnt_p = _oproj_route(
        xp, pm6, pl.BlockSpec((None, 6, 1, d), lambda i: (i // tiles_moe, 0, 0, 0)),
        o_p.reshape(n_p, -1), wo, g2, wrh, wrl, br, tm=tmoe)
    yp = _moe_routed(
        hp, xp, pm6, pl.BlockSpec((None, 6, 1, d), lambda i, j, *_: (i // tiles_moe, 0, 0, 0)),
        route_p, cnt_p, weg, weu, wed, fg, tm=tmoe)

    tc_s, ts_s = _rope_tables(past + jnp.arange(t_s))
    tc_s = jnp.tile(tc_s, (ns_b, 1))
    ts_s = jnp.tile(ts_s, (ns_b, 1))
    wqn = wq_nope.reshape(Q_LORA, -1).astype(BF16)
    wqp = wq_pe.reshape(Q_LORA, -1).astype(BF16)
    wqs = wq_sw.reshape(Q_LORA, -1).astype(BF16)
    wukt = w_uk[0].reshape(KV_LORA, MLA_HEADS, QK_NOPE).transpose(1, 2, 0).astype(BF16)
    wuv_h = w_uv[0].reshape(KV_LORA, MLA_HEADS, V_HD).transpose(1, 0, 2).astype(BF16)
    sm = sample_mod(1, 0)
    ckv_s, kpe_s, qlat, qpe = _mla_sample(
        xs, sm, sspec(n_s), g1, (wdc, gq, gkv, wqn, wqp, wqs, wukt), tc_s, ts_s, tm=n_s)
    o_lat = _attn_sample(
        qlat.reshape(ns_b, t_s * MLA_HEADS, KV_LORA), qpe.reshape(ns_b, t_s * MLA_HEADS, QK_ROPE),
        cache_ckv[0], cache_kpe[0], ckv_s.reshape(ns_b, t_s, KV_LORA),
        kpe_s.reshape(ns_b, t_s, QK_ROPE))
    sm6 = jnp.broadcast_to(mods[1, nb:][:, None], (ns_b, t_s, 6, d)).reshape(n_s, 6, d).transpose(1, 0, 2)
    xs, hs, route_s, cnt_s = _oproj_route(
        xs, sm6, pl.BlockSpec((6, n_s, d), lambda i: (0, i, 0)),
        o_lat.reshape(n_s, MLA_HEADS * KV_LORA), wo, g2, wrh, wrl, br, wuv_h, tm=n_s)
    ys = _moe_routed(
        hs, xs, sm6, pl.BlockSpec((6, n_s, d), lambda i, j, *_: (0, i, 0)),
        route_s, cnt_s, weg, weu, wed, fg, tm=n_s)

    return (yp.reshape(nb, seq, d), ys.reshape(ns_b, t_s, d),
            conv_p[None], conv_s[None], gv_s.reshape(ns_b, t_s, c_ch)[None],
            ckv_p.reshape(nb, seq, KV_LORA)[None], kpe_p.reshape(nb, seq, QK_ROPE)[None],
            ckv_s.reshape(ns_b, t_s, KV_LORA)[None], kpe_s.reshape(ns_b, t_s, QK_ROPE)[None])
```

```python
import functools
import math

import jax
import jax.numpy as jnp
from jax import lax
from jax.experimental import pallas as pl
from jax.experimental.pallas import tpu as pltpu

F32 = jnp.float32
BF16 = jnp.bfloat16

EPS = 1e-6
CHUNK = 64
CONV_W = 31
HIST = CONV_W - 1
HIST_PAD = 32
GMLP_CHUNK = 128
GMLP_HEADS = 8
MLA_HEADS = 8
Q_LORA = 256
KV_LORA = 256
QK_NOPE = 128
QK_ROPE = 64
V_HD = 128
V_AUG = V_HD + 16
QK_CAT = 256
ROPE_BASE = 10000.0
N_EXPERTS = 8
MOE_ROWS = 256
MOE_PAD = 128
LANES = 128
SUBLANES = 8
CONV_UNROLL = True
NEG_BIG = -1e30

VMEM_LIMIT = 56 * 1024 * 1024


def _cparams(*sem):
    return pltpu.CompilerParams(dimension_semantics=sem, vmem_limit_bytes=VMEM_LIMIT)


def _const_spec(shape):
    nd = len(shape)
    return pl.BlockSpec(shape, lambda *_: (0,) * nd, pipeline_mode=pl.Buffered(1))


def _modulate(x, g, shift, scale):
    y = x * lax.rsqrt(jnp.mean(x * x, axis=-1, keepdims=True) + EPS)
    return (y * g) * (1.0 + scale) + shift


def _layer_norm(x, g, b):
    mu = jnp.mean(x, axis=-1, keepdims=True)
    xc = x - mu
    var = jnp.mean(xc * xc, axis=-1, keepdims=True)
    return xc * lax.rsqrt(var + EPS) * g + b


def _dot(a, b):
    return jnp.dot(a, b, preferred_element_type=F32)


_NT = (((1,), (1,)), ((), ()))


def _adaln_kernel(c_ref, w_ref, b_ref, o_ref):
    c = c_ref[...]
    a = jax.nn.silu(c).astype(BF16)
    o_ref[...] = _dot(a, w_ref[...].astype(BF16)) + b_ref[...]


def _adaln(c_all, w_ada, b_ada, tn=1536):
    depth, d, n = w_ada.shape
    rows = c_all.shape[0]
    return pl.pallas_call(
        _adaln_kernel,
        grid=(depth, n // tn),
        in_specs=[
            pl.BlockSpec((rows, d), lambda l, j: (0, 0)),
            pl.BlockSpec((None, d, tn), lambda l, j: (l, 0, j)),
            pl.BlockSpec((None, 1, tn), lambda l, j: (l, 0, j)),
        ],
        out_specs=pl.BlockSpec((None, rows, tn), lambda l, j: (l, 0, j)),
        out_shape=jax.ShapeDtypeStruct((depth, rows, n), F32),
        compiler_params=_cparams("parallel", "parallel"),
        name="adaln",
    )(c_all, w_ada, b_ada.reshape(depth, 1, n))


def _mixer_kernel(*refs, ns, t, carry_hist, conv_rows):
    if carry_hist:
        (x_ref, mod_ref, g_ref, win_ref, bin_ref, cw_ref, cb_ref, lag_ref, lab_ref,
         lvg_ref, lvb_ref, wsp_ref, bsp_ref, woa_ref, wob_ref, bout_ref,
         xo_ref, conv_ref, aext_ref, ya_ref, yb_ref, tapwin_ref) = refs
        hist_ref = v_ref = None
    else:
        (x_ref, mod_ref, g_ref, win_ref, bin_ref, cw_ref, cb_ref, lag_ref, lab_ref,
         lvg_ref, lvb_ref, wsp_ref, bsp_ref, woa_ref, wob_ref, bout_ref, hist_ref,
         xo_ref, conv_ref, v_ref, aext_ref, ya_ref, yb_ref, tapwin_ref) = refs
    c = cw_ref.shape[1]
    rows = ns * t

    x = x_ref[...]
    h = _modulate(x, g_ref[...], mod_ref[0], mod_ref[1]).astype(BF16)
    p = _dot(h, win_ref[...]) + bin_ref[...]

    a = p[:, :c] * jax.nn.sigmoid(p[:, c:2 * c])
    if carry_hist:
        @pl.when(pl.program_id(1) == 0)
        def _():
            aext_ref[:, :HIST_PAD, :] = jnp.zeros((ns, HIST_PAD, c), F32)
    else:
        aext_ref[:, HIST_PAD - HIST:HIST_PAD, :] = hist_ref[...]
    aext_ref[:, HIST_PAD:, :] = a.reshape(ns, t, c)

    off = HIST_PAD - HIST
    n_cc = t // conv_rows
    lag = lag_ref[...]
    lab = lab_ref[...]
    cb = cb_ref[...]

    def finish(acc):
        y = _layer_norm(acc + cb, lag, lab)
        return jax.nn.silu(y).astype(BF16)

    def conv_direct(s):
        acc = jnp.zeros((conv_rows, c), F32)
        for k in range(CONV_W):
            acc = acc + aext_ref[s, off + k:off + k + conv_rows, :] * cw_ref[k:k + 1, :]
        return finish(acc)

    def conv_phased(s, r0):
        acc = jnp.zeros((conv_rows, c), F32)
        for k in range(CONV_W):
            r = (off + k) % SUBLANES
            q8 = pl.multiple_of(r0 + (off + k - r), SUBLANES)
            if r == 0:
                tap = aext_ref[s, pl.ds(q8, conv_rows), :]
            else:
                tap = tapwin_ref[r - 1, pl.ds(q8, conv_rows), :]
            acc = acc + tap * cw_ref[k:k + 1, :]
        return finish(acc)

    for s in range(ns):
        if n_cc == 1:
            ya_ref[s * t:(s + 1) * t, :] = conv_direct(s)
        else:
            span = t + HIST_PAD - SUBLANES
            for r in range(1, SUBLANES):
                tapwin_ref[r - 1] = aext_ref[s, r:r + span, :]

            def body(i, carry, s=s):
                r0 = pl.multiple_of(i * conv_rows, conv_rows)
                ya_ref[pl.ds(s * t + r0, conv_rows), :] = conv_phased(s, r0)
                return carry
            lax.fori_loop(0, n_cc, body, 0, unroll=CONV_UNROLL)

    conv_ref[...] = aext_ref[:, t + HIST_PAD - HIST:t + HIST_PAD, :]
    if carry_hist:
        aext_ref[:, :HIST_PAD, :] = aext_ref[:, t:t + HIST_PAD, :]

    z = jax.nn.gelu(p[:, 2 * c:])
    u = z[:, :c]
    v = _layer_norm(z[:, c:], lvg_ref[...], lvb_ref[...])
    if v_ref is not None:
        v_ref[...] = v
    ri = lax.broadcasted_iota(jnp.int32, (GMLP_CHUNK, GMLP_CHUNK), 0)
    ci = lax.broadcasted_iota(jnp.int32, (GMLP_CHUNK, GMLP_CHUNK), 1)
    tril = ri >= ci
    low_half = ci < (LANES // 2)
    w_heads = [jnp.where(tril, wsp_ref[hh], 0.0).astype(BF16) for hh in range(GMLP_HEADS)]
    for blk in range(rows // GMLP_CHUNK):
        r = slice(blk * GMLP_CHUNK, (blk + 1) * GMLP_CHUNK)
        for j in range(c // LANES):
            l = slice(j * LANES, (j + 1) * LANES)
            vp = v[r, l]
            lo = jnp.where(low_half, vp, 0.0).astype(BF16)
            hi = jnp.where(low_half, 0.0, vp).astype(BF16)
            sp = _dot(w_heads[2 * j], lo) + _dot(w_heads[2 * j + 1], hi)
            yb_ref[r, l] = (u[r, l] * (sp + bsp_ref[:, l])).astype(BF16)

    out = _dot(ya_ref[...], woa_ref[...]) + _dot(yb_ref[...], wob_ref[...]) + bout_ref[...]
    xo_ref[...] = x + mod_ref[2] * out


def _mixer(x, mod, mod_spec, g1, wts, *, ns, t, n_outer, n_inner, hist=None):
    (w_in, b_in, cw, cb, lag, lab, lvg, lvb, wsp, bsp, woa, wob, b_out) = wts
    d = x.shape[1]
    c = cw.shape[1]
    rows = ns * t
    carry_hist = hist is None
    tile = lambda o, i: (o * n_inner + i, 0)
    in_specs = [
        pl.BlockSpec((rows, d), tile),
        mod_spec,
        _const_spec((1, d)),
        _const_spec(w_in.shape), _const_spec(b_in.shape),
        _const_spec(cw.shape), _const_spec(cb.shape),
        _const_spec(lag.shape), _const_spec(lab.shape),
        _const_spec(lvg.shape), _const_spec(lvb.shape),
        _const_spec(wsp.shape), _const_spec(bsp.shape),
        _const_spec(woa.shape), _const_spec(wob.shape), _const_spec(b_out.shape),
    ]
    args = [x, mod, g1, w_in, b_in, cw, cb, lag, lab, lvg, lvb, wsp, bsp, woa, wob, b_out]
    n_tiles = n_outer * n_inner
    out_shape = [jax.ShapeDtypeStruct(x.shape, F32)]
    out_specs = [pl.BlockSpec((rows, d), tile)]
    if carry_hist:
        out_shape.append(jax.ShapeDtypeStruct((n_outer * ns, HIST, c), F32))
        out_specs.append(pl.BlockSpec((ns, HIST, c), lambda o, i: (o, 0, 0)))
    else:
        in_specs.append(pl.BlockSpec((ns, HIST, c), lambda o, i: (o * n_inner + i, 0, 0)))
        args.append(hist)
        out_shape.append(jax.ShapeDtypeStruct((n_tiles * ns, HIST, c), F32))
        out_specs.append(pl.BlockSpec((ns, HIST, c), lambda o, i: (o * n_inner + i, 0, 0)))
        out_shape.append(jax.ShapeDtypeStruct((x.shape[0], c), F32))
        out_specs.append(pl.BlockSpec((rows, c), tile))
    conv_rows = min(t, 64)
    return pl.pallas_call(
        functools.partial(_mixer_kernel, ns=ns, t=t, carry_hist=carry_hist, conv_rows=conv_rows),
        grid=(n_outer, n_inner),
        in_specs=in_specs,
        out_specs=out_specs,
        out_shape=out_shape,
        scratch_shapes=[
            pltpu.VMEM((ns, t + HIST_PAD, c), F32),
            pltpu.VMEM((rows, c), BF16),
            pltpu.VMEM((rows, c), BF16),
            pltpu.VMEM((SUBLANES - 1, t + HIST_PAD - SUBLANES, c), F32),
        ],
        compiler_params=_cparams("arbitrary", "arbitrary"),
        name="mixer_prompt" if carry_hist else "mixer_sample",
    )(*args)


def _ffn_kernel(x_ref, mod_ref, g_ref, wg_ref, wu_ref, wd_ref, o_ref, *, n_chunks):
    x = x_ref[...]
    h = _modulate(x, g_ref[...], mod_ref[0], mod_ref[1]).astype(BF16)
    ff = wg_ref.shape[1]
    cw = ff // n_chunks
    acc = None
    for j in range(n_chunks):
        sl = slice(j * cw, (j + 1) * cw)
        gate = _dot(h, wg_ref[:, sl])
        up = _dot(h, wu_ref[:, sl])
        act = (jax.nn.silu(gate) * up).astype(BF16)
        part = _dot(act, wd_ref[sl, :])
        acc = part if acc is None else acc + part
    o_ref[...] = x + mod_ref[2] * acc


def _ffn(x, mod, mod_spec, g2, wg, wu, wd, *, tm):
    n, d = x.shape
    return pl.pallas_call(
        functools.partial(_ffn_kernel, n_chunks=2),
        grid=(n // tm,),
        in_specs=[
            pl.BlockSpec((tm, d), lambda i: (i, 0)),
            mod_spec,
            _const_spec((1, d)),
            _const_spec(wg.shape), _const_spec(wu.shape), _const_spec(wd.shape),
        ],
        out_specs=pl.BlockSpec((tm, d), lambda i: (i, 0)),
        out_shape=jax.ShapeDtypeStruct((n, d), F32),
        compiler_params=_cparams("parallel"),
        name="ffn",
    )(x, mod, g2, wg, wu, wd)


def _mla_latents(x_ref, mod_ref, g_ref, wdc_ref, gq_ref, gkv_ref, tc, ts):
    x = x_ref[...]
    h = _modulate(x, g_ref[...], mod_ref[0], mod_ref[1]).astype(BF16)
    p = _dot(h, wdc_ref[...])
    cq = p[:, :Q_LORA]
    cq = cq * lax.rsqrt(jnp.mean(cq * cq, axis=-1, keepdims=True) + EPS) * gq_ref[...]
    ckv = p[:, Q_LORA:Q_LORA + KV_LORA]
    ckv = ckv * lax.rsqrt(jnp.mean(ckv * ckv, axis=-1, keepdims=True) + EPS) * gkv_ref[...]
    b0 = Q_LORA + KV_LORA
    kpe = p[:, b0:b0 + LANES] * tc + p[:, b0 + LANES:b0 + 2 * LANES] * ts
    return cq.astype(BF16), ckv, kpe


def _mla_prompt_kernel(x_ref, mod_ref, g_ref, wdc_ref, gq_ref, gkv_ref, tc_ref, ts_ref,
                       wqa_ref, wqb_ref, wuk_ref, wuvt_ref,
                       ckv_ref, kpe_ref, q_ref, k_ref, vt_ref):
    tc = tc_ref[...]
    ts = ts_ref[...]
    cq, ckv, kpe = _mla_latents(x_ref, mod_ref, g_ref, wdc_ref, gq_ref, gkv_ref, tc, ts)
    ckv_ref[...] = ckv
    kpe_ref[...] = kpe[:, :QK_ROPE]
    qa = _dot(cq, wqa_ref[...])
    qb = _dot(cq, wqb_ref[...])
    ckv_b = ckv.astype(BF16)
    kn = _dot(ckv_b, wuk_ref[...])
    vt = lax.dot_general(wuvt_ref[...], ckv_b, _NT, preferred_element_type=F32).astype(BF16)
    tm = vt.shape[1]
    pad_rows = V_AUG - V_HD
    ones_row = jnp.where(lax.broadcasted_iota(jnp.int32, (pad_rows, tm), 0) == 0, 1.0, 0.0).astype(BF16)
    for hh in range(MLA_HEADS):
        vt_ref[hh * V_AUG:hh * V_AUG + V_HD, :] = vt[hh * V_HD:(hh + 1) * V_HD, :]
        vt_ref[hh * V_AUG + V_HD:(hh + 1) * V_AUG, :] = ones_row
    kpe_b = kpe.astype(BF16)
    for hh in range(MLA_HEADS):
        base = hh * QK_CAT
        q_ref[:, base:base + QK_NOPE] = qa[:, base:base + QK_NOPE].astype(BF16)
        q_ref[:, base + QK_NOPE:base + QK_CAT] = (
            qa[:, base + QK_NOPE:base + QK_CAT] * tc
            + qb[:, hh * LANES:(hh + 1) * LANES] * ts).astype(BF16)
        k_ref[:, base:base + QK_NOPE] = kn[:, hh * QK_NOPE:(hh + 1) * QK_NOPE].astype(BF16)
        k_ref[:, base + QK_NOPE:base + QK_CAT] = kpe_b


def _mla_prompt(x, mod, mod_spec, g1, wts, tc, ts, *, tm, tiles_per_seq):
    wdc, gq, gkv, wqa, wqb, wuk, wuvt = wts
    n, d = x.shape
    hq = MLA_HEADS * QK_CAT
    hv = MLA_HEADS * V_AUG
    nb = n // (tm * tiles_per_seq)
    row = lambda i: (i, 0)
    pos = lambda i: (i % tiles_per_seq, 0)
    return pl.pallas_call(
        _mla_prompt_kernel,
        grid=(n // tm,),
        in_specs=[
            pl.BlockSpec((tm, d), row), mod_spec, _const_spec((1, d)),
            _const_spec(wdc.shape), _const_spec(gq.shape), _const_spec(gkv.shape),
            pl.BlockSpec((tm, LANES), pos), pl.BlockSpec((tm, LANES), pos),
            _const_spec(wqa.shape), _const_spec(wqb.shape),
            _const_spec(wuk.shape), _const_spec(wuvt.shape),
        ],
        out_specs=[
            pl.BlockSpec((tm, KV_LORA), row), pl.BlockSpec((tm, QK_ROPE), row),
            pl.BlockSpec((tm, hq), row), pl.BlockSpec((tm, hq), row),
            pl.BlockSpec((None, hv, tm), lambda i: (i // tiles_per_seq, 0, i % tiles_per_seq)),
        ],
        out_shape=[
            jax.ShapeDtypeStruct((n, KV_LORA), F32), jax.ShapeDtypeStruct((n, QK_ROPE), F32),
            jax.ShapeDtypeStruct((n, hq), BF16), jax.ShapeDtypeStruct((n, hq), BF16),
            jax.ShapeDtypeStruct((nb, hv, tm * tiles_per_seq), BF16),
        ],
        compiler_params=_cparams("parallel"),
        name="mla_proj_prompt",
    )(x, mod, g1, wdc, gq, gkv, tc, ts, wqa, wqb, wuk, wuvt)


def _mla_sample_kernel(x_ref, mod_ref, g_ref, wdc_ref, gq_ref, gkv_ref, tc_ref, ts_ref,
                       wqn_ref, wqp_ref, wqs_ref, wukt_ref,
                       ckv_ref, kpe_ref, qlat_ref, qpe_ref):
    tc = tc_ref[...]
    ts = ts_ref[...]
    cq, ckv, kpe = _mla_latents(x_ref, mod_ref, g_ref, wdc_ref, gq_ref, gkv_ref, tc, ts)
    ckv_ref[...] = ckv
    kpe_ref[...] = kpe[:, :QK_ROPE]
    qn = _dot(cq, wqn_ref[...]).astype(BF16)
    qp = _dot(cq, wqp_ref[...])
    qs = _dot(cq, wqs_ref[...])
    for j in range(MLA_HEADS * QK_ROPE // LANES):
        l = slice(j * LANES, (j + 1) * LANES)
        qpe_ref[:, l] = (qp[:, l] * tc + qs[:, l] * ts).astype(BF16)
    for hh in range(MLA_HEADS):
        qlat_ref[:, hh * KV_LORA:(hh + 1) * KV_LORA] = _dot(
            qn[:, hh * QK_NOPE:(hh + 1) * QK_NOPE], wukt_ref[hh]).astype(BF16)


def _mla_sample(x, mod, mod_spec, g1, wts, tc, ts, *, tm):
    wdc, gq, gkv, wqn, wqp, wqs, wukt = wts
    n, d = x.shape
    row = lambda i: (i, 0)
    return pl.pallas_call(
        _mla_sample_kernel,
        grid=(n // tm,),
        in_specs=[
            pl.BlockSpec((tm, d), row), mod_spec, _const_spec((1, d)),
            _const_spec(wdc.shape), _const_spec(gq.shape), _const_spec(gkv.shape),
            pl.BlockSpec((tm, LANES), row), pl.BlockSpec((tm, LANES), row),
            _const_spec(wqn.shape), _const_spec(wqp.shape), _const_spec(wqs.shape),
            _const_spec(wukt.shape),
        ],
        out_specs=[
            pl.BlockSpec((tm, KV_LORA), row), pl.BlockSpec((tm, QK_ROPE), row),
            pl.BlockSpec((tm, MLA_HEADS * KV_LORA), row),
            pl.BlockSpec((tm, MLA_HEADS * QK_ROPE), row),
        ],
        out_shape=[
            jax.ShapeDtypeStruct((n, KV_LORA), F32), jax.ShapeDtypeStruct((n, QK_ROPE), F32),
            jax.ShapeDtypeStruct((n, MLA_HEADS * KV_LORA), BF16),
            jax.ShapeDtypeStruct((n, MLA_HEADS * QK_ROPE), BF16),
        ],
        compiler_params=_cparams("parallel"),
        name="mla_proj_sample",
    )(x, mod, g1, wdc, gq, gkv, tc, ts, wqn, wqp, wqs, wukt)


def _attn_prompt_kernel(q_ref, k_ref, vt_ref, o_ref, s_scr, p_scr, acc_scr, *, tq, tk):
    qi = pl.program_id(2)
    n_full = 2 * qi

    def qk_to(slot, t):
        k = k_ref[pl.ds(pl.multiple_of(t * tk, tk), tk), :]
        s = lax.dot_general(k, q_ref[...], _NT, preferred_element_type=F32)
        s_scr[slot] = s
        return jnp.max(s, axis=0, keepdims=True)

    def softmax_from(slot, m, bm):
        m_new = jnp.maximum(m, bm)
        alpha = jnp.exp2(m - m_new)
        p_scr[slot] = jnp.exp2(s_scr[slot] - m_new).astype(BF16)
        return m_new, alpha

    def pv_from(slot, t, alpha):
        vt = vt_ref[:, pl.ds(pl.multiple_of(t * tk, tk), tk)]
        acc_scr[...] = alpha * acc_scr[...] + _dot(vt, p_scr[slot])

    p_scr[1] = jnp.zeros((tk, tq), BF16)
    acc_scr[...] = jnp.zeros((V_AUG, tq), F32)
    m0 = jnp.full((1, tq), NEG_BIG, F32)
    bm0 = qk_to(0, 0)

    def body(i, carry):
        m, bm, a_prev = carry
        t = 2 * i
        bm1 = qk_to(1, t + 1)
        m, a0 = softmax_from(0, m, bm)
        pv_from(1, jnp.maximum(t - 1, 0), a_prev)
        bm2 = qk_to(0, t + 2)
        m, a1 = softmax_from(1, m, bm1)
        pv_from(0, t, a0)
        return m, bm2, a1

    m, bm, a_prev = lax.fori_loop(0, qi, body, (m0, bm0, jnp.ones((1, tq), F32)))
    pv_from(1, jnp.maximum(n_full - 1, 0), a_prev)
    kk = lax.broadcasted_iota(jnp.int32, (tk, tk), 0) // CHUNK
    qq = lax.broadcasted_iota(jnp.int32, (tk, tk), 1) // CHUNK
    vis = qq >= kk
    s_a = jnp.where(vis, s_scr[0, :, :tk], NEG_BIG)
    bm_d = jnp.concatenate([jnp.max(s_a, axis=0, keepdims=True), bm[:, tk:]], axis=1)
    m_new = jnp.maximum(m, bm_d)
    alpha = jnp.exp2(m - m_new)
    p_a = jnp.exp2(s_a - m_new[:, :tk])
    p_b = jnp.exp2(s_scr[0, :, tk:] - m_new[:, tk:])
    p_scr[0, :, :tk] = p_a.astype(BF16)
    p_scr[0, :, tk:] = p_b.astype(BF16)
    pv_from(0, n_full, alpha)
    k1 = k_ref[pl.ds(pl.multiple_of((n_full + 1) * tk, tk), tk), :]
    s1 = lax.dot_general(k1, q_ref[tk:, :], _NT, preferred_element_type=F32)
    s1 = jnp.where(vis, s1, NEG_BIG)
    m_b = m_new[:, tk:]
    m_b2 = jnp.maximum(m_b, jnp.max(s1, axis=0, keepdims=True))
    a1 = jnp.exp2(m_b - m_b2)
    p1 = jnp.exp2(s1 - m_b2)
    vt1 = vt_ref[:, pl.ds(pl.multiple_of((n_full + 1) * tk, tk), tk)]
    acc_b = a1 * acc_scr[:, tk:] + _dot(vt1, p1.astype(BF16))
    o_ref[:tk, :] = (acc_scr[:V_HD, :tk] / acc_scr[V_HD:V_HD + 1, :tk]).T.astype(o_ref.dtype)
    o_ref[tk:, :] = (acc_b[:V_HD] / acc_b[V_HD:V_HD + 1]).T.astype(o_ref.dtype)


def _attn_prompt(q, k, vt, *, tq):
    b, s, _ = q.shape
    tk = tq // 2
    return pl.pallas_call(
        functools.partial(_attn_prompt_kernel, tq=tq, tk=tk),
        grid=(b, MLA_HEADS, s // tq),
        in_specs=[
            pl.BlockSpec((None, tq, QK_CAT), lambda bi, hi, qi: (bi, qi, hi)),
            pl.BlockSpec((None, s, QK_CAT), lambda bi, hi, qi: (bi, 0, hi)),
            pl.BlockSpec((None, V_AUG, s), lambda bi, hi, qi: (bi, hi, 0)),
        ],
        out_specs=pl.BlockSpec((None, tq, V_HD), lambda bi, hi, qi: (bi, qi, hi)),
        out_shape=jax.ShapeDtypeStruct((b, s, MLA_HEADS * V_HD), BF16),
        scratch_shapes=[pltpu.VMEM((2, tk, tq), F32), pltpu.VMEM((2, tk, tq), BF16),
                        pltpu.VMEM((V_AUG, tq), F32)],
        compiler_params=_cparams("parallel", "parallel", "arbitrary"),
        name="attn_prompt",
    )(q, k, vt)


def _attn_sample_kernel(ql_ref, qp_ref, cc_ref, cp_ref, nc_ref, np_ref, o_ref):
    ql = ql_ref[...]
    qp = qp_ref[...]
    nt = (((1,), (1,)), ((), ()))
    cc = cc_ref[...].astype(BF16)
    cp = cp_ref[...].astype(BF16)
    nc = nc_ref[...].astype(BF16)
    npe = np_ref[...].astype(BF16)
    s_c = (lax.dot_general(ql, cc, nt, preferred_element_type=F32)
           + lax.dot_general(qp, cp, nt, preferred_element_type=F32))
    s_n = (lax.dot_general(ql, nc, nt, preferred_element_type=F32)
           + lax.dot_general(qp, npe, nt, preferred_element_type=F32))
    m = jnp.maximum(jnp.max(s_c, axis=-1, keepdims=True), jnp.max(s_n, axis=-1, keepdims=True))
    p_c = jnp.exp2(s_c - m)
    p_n = jnp.exp2(s_n - m)
    l = jnp.sum(p_c, axis=-1, keepdims=True) + jnp.sum(p_n, axis=-1, keepdims=True)
    o = _dot(p_c.astype(BF16), cc) + _dot(p_n.astype(BF16), nc)
    o_ref[...] = (o / l).astype(o_ref.dtype)


def _attn_sample(qlat, qpe, cache_ckv, cache_kpe, ckv_new, kpe_new):
    nb, past, _ = cache_ckv.shape
    r = qlat.shape[1]
    t = ckv_new.shape[1]
    blk = lambda shape: pl.BlockSpec((None,) + shape, lambda i: (i, 0, 0))
    return pl.pallas_call(
        _attn_sample_kernel,
        grid=(nb,),
        in_specs=[blk((r, KV_LORA)), blk((r, QK_ROPE)), blk((past, KV_LORA)),
                  blk((past, QK_ROPE)), blk((t, KV_LORA)), blk((t, QK_ROPE))],
        out_specs=blk((r, KV_LORA)),
        out_shape=jax.ShapeDtypeStruct((nb, r, KV_LORA), BF16),
        compiler_params=_cparams("parallel"),
        name="attn_sample",
    )(qlat, qpe, cache_ckv, cache_kpe, ckv_new, kpe_new)


def _route_tail(xm, mod_ref, g_ref, wrc_ref, br_ref, xo_ref, h_ref, route_ref, cnt_ref):
    tm = xm.shape[0]
    xo_ref[...] = xm
    hf = _modulate(xm, g_ref[...], mod_ref[3], mod_ref[4])
    hb = hf.astype(BF16)
    h_ref[...] = hb
    h_lo = (hf - hb.astype(F32)).astype(BF16)
    both = _dot(hb, wrc_ref[...])
    logits = (both[:, :LANES] + both[:, LANES:] + _dot(h_lo, wrc_ref[:, :LANES])) + br_ref[...]
    lane = lax.broadcasted_iota(jnp.int32, (tm, LANES), 1)
    m1 = jnp.max(logits, axis=-1, keepdims=True)
    i1 = jnp.min(jnp.where(logits == m1, lane, LANES), axis=-1, keepdims=True)
    rest = jnp.where(lane == i1, -jnp.inf, logits)
    m2 = jnp.max(rest, axis=-1, keepdims=True)
    i2 = jnp.min(jnp.where(rest == m2, lane, LANES), axis=-1, keepdims=True)
    e2 = jnp.exp(m2 - m1)
    den = 1.0 + e2
    sel1 = lane == i1
    sel2 = lane == i2
    onehot = jnp.where(sel1, 1.0, jnp.where(sel2, 1.0, 0.0))
    ri = lax.broadcasted_iota(jnp.int32, (tm, tm), 0)
    ci = lax.broadcasted_iota(jnp.int32, (tm, tm), 1)
    before = jnp.where(ri > ci, 1.0, 0.0).astype(BF16)
    rank = _dot(before, onehot.astype(BF16))
    cnt = jnp.sum(onehot, axis=0, keepdims=True)
    nblk = jnp.floor((cnt + (MOE_PAD - 1)) * (1.0 / MOE_PAD))
    r8 = lax.broadcasted_iota(jnp.int32, (LANES, LANES), 0)
    c8 = lax.broadcasted_iota(jnp.int32, (LANES, LANES), 1)
    upper = jnp.where(r8 < c8, 1.0, 0.0).astype(BF16)
    first_blk = _dot(jnp.broadcast_to(nblk, (8, LANES)).astype(BF16), upper)[0:1]
    slot = first_blk * MOE_PAD + rank
    d1 = jnp.sum(jnp.where(sel1, slot, 0.0), axis=-1, keepdims=True)
    d2 = jnp.sum(jnp.where(sel2, slot, 0.0), axis=-1, keepdims=True)
    route_ref[...] = jnp.where(lane == 0, d1, jnp.where(lane == 1, d2, jnp.where(
        lane == 2, 1.0 / den, jnp.where(lane == 3, e2 / den, 0.0))))
    cnt_ref[...] = jnp.broadcast_to(cnt, (8, LANES))


def _oproj_route_kernel(x_ref, mod_ref, o_ref, wo_ref, g_ref, wrc_ref, br_ref,
                        xo_ref, h_ref, route_ref, cnt_ref):
    xm = x_ref[...] + mod_ref[2] * _dot(o_ref[...], wo_ref[...])
    _route_tail(xm, mod_ref, g_ref, wrc_ref, br_ref, xo_ref, h_ref, route_ref, cnt_ref)


def _oproj_latent_route_kernel(x_ref, mod_ref, o_ref, wuv_ref, wo_ref, g_ref, wrc_ref,
                               br_ref, xo_ref, h_ref, route_ref, cnt_ref):
    acc = None
    for hh in range(MLA_HEADS):
        oh = _dot(o_ref[:, hh * KV_LORA:(hh + 1) * KV_LORA], wuv_ref[hh]).astype(BF16)
        part = _dot(oh, wo_ref[hh * V_HD:(hh + 1) * V_HD, :])
        acc = part if acc is None else acc + part
    xm = x_ref[...] + mod_ref[2] * acc
    _route_tail(xm, mod_ref, g_ref, wrc_ref, br_ref, xo_ref, h_ref, route_ref, cnt_ref)


def _oproj_route(x, mod, mod_spec, o, wo, g2, wrc, br, wuv=None, *, tm):
    n, d = x.shape
    row = lambda i: (i, 0)
    in_specs = [pl.BlockSpec((tm, d), row), mod_spec, pl.BlockSpec((tm, o.shape[1]), row)]
    args = [x, mod, o]
    if wuv is not None:
        in_specs.append(_const_spec(wuv.shape))
        args.append(wuv)
    in_specs += [_const_spec(wo.shape), _const_spec((1, d)), _const_spec(wrc.shape),
                 _const_spec(br.shape)]
    args += [wo, g2, wrc, br]
    return pl.pallas_call(
        _oproj_route_kernel if wuv is None else _oproj_latent_route_kernel,
        grid=(n // tm,),
        in_specs=in_specs,
        out_specs=[pl.BlockSpec((tm, d), row), pl.BlockSpec((tm, d), row),
                   pl.BlockSpec((tm, LANES), row), pl.BlockSpec((None, 8, LANES), lambda i: (i, 0, 0))],
        out_shape=[jax.ShapeDtypeStruct((n, d), F32), jax.ShapeDtypeStruct((n, d), BF16),
                   jax.ShapeDtypeStruct((n, LANES), F32),
                   jax.ShapeDtypeStruct((n // tm, 8, LANES), F32)],
        compiler_params=_cparams("parallel"),
        name="oproj_route" if wuv is None else "oproj_latent_route",
    )(*args)


def _moe_routed_kernel(row0_ref, npad_ref, h_ref, x_ref, mod_ref, rc_ref, rr_ref,
                       wg_ref, wu_ref, wd_ref, fg_ref, y_ref, acc_ref):
    i = pl.program_id(0)
    e = pl.program_id(1)
    n_exp = pl.num_programs(1)
    tm = h_ref.shape[0]
    start = row0_ref[i * n_exp + e]
    n_pad = npad_ref[i * n_exp + e]
    per = MOE_ROWS // MOE_PAD
    has_head = n_pad % per

    @pl.when(e == 0)
    def _():
        acc_ref[...] = jnp.zeros(acc_ref.shape, F32)

    def block(rows, first_slot):
        base = first_slot.astype(F32)
        rr = rr_ref[...]
        srow = lax.broadcasted_iota(jnp.int32, (rows, tm), 0).astype(F32) + base
        hit1 = rr[0:1, :] == srow
        hit2 = rr[1:2, :] == srow
        sel = jnp.where(hit1, 1.0, jnp.where(hit2, 1.0, 0.0)).astype(BF16)
        gate = jnp.sum(jnp.where(hit1, rr[2:3, :], jnp.where(hit2, rr[3:4, :], 0.0)),
                       axis=-1, keepdims=True)
        hb = _dot(sel, h_ref[...]).astype(BF16)
        act = (jax.nn.silu(_dot(hb, wg_ref[...])) * _dot(hb, wu_ref[...])).astype(BF16)
        yb = (_dot(act, wd_ref[...]) * gate).astype(BF16)
        rc = rc_ref[...]
        scol = lax.broadcasted_iota(jnp.int32, (tm, rows), 1).astype(F32) + base
        sel_t = jnp.where(rc[:, 0:1] == scol, 1.0,
                          jnp.where(rc[:, 1:2] == scol, 1.0, 0.0)).astype(BF16)
        acc_ref[...] += _dot(sel_t, yb)

    @pl.when(has_head == 1)
    def _():
        block(MOE_PAD, start)

    def body(b, carry):
        block(MOE_ROWS, start + has_head * MOE_PAD + b * MOE_ROWS)
        return carry

    lax.fori_loop(0, n_pad // per, body, 0)

    @pl.when(e == n_exp - 1)
    def _():
        xn = x_ref[...] + mod_ref[5] * acc_ref[...]
        y_ref[...] = xn * lax.rsqrt(jnp.mean(xn * xn, axis=-1, keepdims=True) + EPS) * fg_ref[...]


def _moe_routed(h, x, mod, mod_spec2, route, cnt, wg, wu, wd, fg, *, tm):
    n, d = x.shape
    ne, _, ff = wg.shape
    n_tiles = n // tm
    n_pad = jnp.ceil(cnt[:, 0, :ne] / MOE_PAD).astype(jnp.int32)
    row0 = (jnp.cumsum(n_pad, axis=1) - n_pad) * MOE_PAD
    route_row = route[:, :8].reshape(n_tiles, tm, 8).transpose(0, 2, 1)
    wspec = lambda shape: pl.BlockSpec((None,) + shape, lambda i, e, r0, npd: (e, 0, 0))
    once = lambda shape: pl.BlockSpec(shape, lambda i, e, r0, npd: (i, 0), pipeline_mode=pl.Buffered(1))
    return pl.pallas_call(
        _moe_routed_kernel,
        grid_spec=pltpu.PrefetchScalarGridSpec(
            num_scalar_prefetch=2,
            grid=(n_tiles, ne),
            in_specs=[
                pl.BlockSpec((tm, d), lambda i, e, r0, npd: (i, 0)), once((tm, d)), mod_spec2,
                pl.BlockSpec((tm, LANES), lambda i, e, r0, npd: (i, 0)),
                pl.BlockSpec((None, 8, tm), lambda i, e, r0, npd: (i, 0, 0)),
                wspec((d, ff)), wspec((d, ff)), wspec((ff, d)),
                pl.BlockSpec((1, d), lambda i, e, r0, npd: (0, 0)),
            ],
            out_specs=pl.BlockSpec((tm, d), lambda i, e, r0, npd: (i, 0)),
            scratch_shapes=[pltpu.VMEM((tm, d), F32)],
        ),
        out_shape=jax.ShapeDtypeStruct((n, d), F32),
        compiler_params=_cparams("parallel", "arbitrary"),
        name="moe_routed",
    )(row0.reshape(-1), n_pad.reshape(-1), h, x, mod, route, route_row, wg, wu, wd, fg)


def _rope_tables(pos):
    half = QK_ROPE // 2
    inv = jnp.exp(-math.log(ROPE_BASE) * jnp.arange(half, dtype=F32) / half)
    ang = pos.astype(F32)[:, None] * inv[None, :]
    cos, sin = jnp.cos(ang), jnp.sin(ang)
    reps = LANES // QK_ROPE
    tc = jnp.tile(jnp.concatenate([cos, cos], axis=-1), (1, reps))
    ts = jnp.tile(jnp.concatenate([-sin, sin], axis=-1), (1, reps))
    return tc, ts


def _swap_halves(w):
    half = w.shape[-1] // 2
    return jnp.concatenate([w[..., half:], w[..., :half]], axis=-1)


def kernel(x_prompt, x_sample, c_prompt, c_sample, state_conv, cache_ckv, cache_kpe, norm1_g, norm2_g, w_ada, b_ada, w_in_ab, b_in_ab, conv_w, conv_b, ln_conv_g, ln_conv_b, ln_v_g, ln_v_b, w_spatial, b_spatial, w_out_ab, b_out_ab, w_ffn_gu, w_ffn_down, w_dc, g_q, g_kv, w_uq, w_uk, w_uv, w_o, w_router, b_router, w_exp_gu, w_exp_down, final_g):
    nb, seq, d = x_prompt.shape
    ns_b, t_s, _ = x_sample.shape
    past = cache_ckv.shape[2]
    c_ch = conv_w.shape[-1]
    n_p = nb * seq
    n_s = ns_b * t_s
    tm = 512
    tq = 1024
    tmoe = 1024
    tiles_per_seq = seq // tm
    s_group = GMLP_CHUNK // t_s

    xp = x_prompt.reshape(n_p, d)
    xs = x_sample.reshape(n_s, d)

    mods = _adaln(jnp.concatenate([c_prompt, c_sample], axis=0), w_ada, b_ada)
    depth = w_ada.shape[0]
    mods = mods.reshape(depth, nb + ns_b, 6, d)

    def prompt_mod(l, k0):
        return mods[l, :nb, k0:k0 + 3][:, :, None, :]

    def sample_mod(l, k0):
        m = mods[l, nb:, k0:k0 + 3]
        m = jnp.broadcast_to(m[:, None], (ns_b, t_s, 3, d)).reshape(n_s, 3, d)
        return m.transpose(1, 0, 2)

    def pspec(tiles_per_b):
        return pl.BlockSpec((None, 3, 1, d), lambda i: (i // tiles_per_b, 0, 0, 0))

    def sspec(rows):
        return pl.BlockSpec((3, rows, d), lambda i: (0, i, 0))

    g1 = norm1_g[0][None]
    g2 = norm2_g[0][None]
    w_in = w_in_ab[0].astype(BF16)
    woa = w_out_ab[0][:c_ch].astype(BF16)
    wob = w_out_ab[0][c_ch:].astype(BF16)
    hd = c_ch // GMLP_HEADS
    bsp_p = jnp.repeat(b_spatial[0].T, hd, axis=1)
    wsp_p = w_spatial[0]
    eye = jnp.eye(s_group, dtype=F32)
    wsp_s = jnp.einsum('ab,hts->hatbs', eye, w_spatial[0][:, :t_s, :t_s]).reshape(
        GMLP_HEADS, GMLP_CHUNK, GMLP_CHUNK)
    bsp_s = jnp.tile(jnp.repeat(b_spatial[0][:, :t_s].T, hd, axis=1), (s_group, 1))
    common = (w_in, b_in_ab[0][None], conv_w[0], conv_b[0][None], ln_conv_g[0][None],
              ln_conv_b[0][None], ln_v_g[0][None], ln_v_b[0][None])
    tail = (woa, wob, b_out_ab[0][None])

    pm = prompt_mod(0, 0)
    xp, conv_p = _mixer(
        xp, pm, pl.BlockSpec((None, 3, 1, d), lambda o, i: (o, 0, 0, 0)), g1,
        common + (wsp_p, bsp_p) + tail, ns=1, t=tm, n_outer=nb, n_inner=tiles_per_seq)
    sm = sample_mod(0, 0)
    n_st = n_s // GMLP_CHUNK
    xs, conv_s, gv_s = _mixer(
        xs, sm, pl.BlockSpec((3, GMLP_CHUNK, d), lambda o, i: (0, o * n_st + i, 0)), g1,
        common + (wsp_s, bsp_s) + tail, ns=s_group, t=t_s, n_outer=1, n_inner=n_st,
        hist=state_conv[0])

    ff = w_ffn_gu.shape[-1] // 2
    wg = w_ffn_gu[0][:, :ff].astype(BF16)
    wu = w_ffn_gu[0][:, ff:].astype(BF16)
    wd = w_ffn_down[0].astype(BF16)
    xp = _ffn(xp, prompt_mod(0, 3), pspec(tiles_per_seq), g2, wg, wu, wd, tm=tm)
    xs = _ffn(xs, sample_mod(0, 3), sspec(n_s), g2, wg, wu, wd, tm=n_s)

    g1 = norm1_g[1][None]
    g2 = norm2_g[1][None]
    scale = (QK_NOPE + QK_ROPE) ** -0.5 * math.log2(math.e)
    zeros64 = jnp.zeros((d, QK_ROPE), F32)
    w_kpe = w_dc[0][:, Q_LORA + KV_LORA:]
    wdc = jnp.concatenate([w_dc[0][:, :Q_LORA + KV_LORA], w_kpe, zeros64,
                           _swap_halves(w_kpe), zeros64], axis=1).astype(BF16)
    wq3 = w_uq[0].reshape(Q_LORA, MLA_HEADS, QK_NOPE + QK_ROPE) * scale
    wq_nope = wq3[:, :, :QK_NOPE]
    wq_pe = wq3[:, :, QK_NOPE:]
    wq_sw = _swap_halves(wq_pe)
    zpad = jnp.zeros((Q_LORA, MLA_HEADS, QK_ROPE), F32)
    wqa = jnp.concatenate([wq_nope, wq_pe, zpad], axis=-1).reshape(Q_LORA, -1).astype(BF16)
    wqb = jnp.concatenate([wq_sw, zpad], axis=-1).reshape(Q_LORA, -1).astype(BF16)
    wuk = w_uk[0].astype(BF16)
    wuvt = w_uv[0].T.astype(BF16)
    gq = g_q[0][None]
    gkv = g_kv[0][None]
    wo = w_o[0].astype(BF16)
    ne = w_router.shape[-1]
    eff = w_exp_gu.shape[-1] // 2
    wr = jnp.zeros((d, LANES), F32).at[:, :ne].set(w_router[0])
    wrh = wr.astype(BF16)
    wrc = jnp.concatenate([wrh, (wr - wrh.astype(F32)).astype(BF16)], axis=1)
    br = jnp.full((1, LANES), NEG_BIG, F32).at[0, :ne].set(b_router[0])
    weg = w_exp_gu[0][:, :, :eff].astype(BF16)
    weu = w_exp_gu[0][:, :, eff:].astype(BF16)
    wed = w_exp_down[0].astype(BF16)
    fg = final_g[None]

    tc_p, ts_p = _rope_tables(jnp.arange(seq))
    pm = prompt_mod(1, 0)
    ckv_p, kpe_p, q_p, k_p, vt_p = _mla_prompt(
        xp, pm, pspec(tiles_per_seq), g1, (wdc, gq, gkv, wqa, wqb, wuk, wuvt),
        tc_p, ts_p, tm=tm, tiles_per_seq=tiles_per_seq)
    o_p = _attn_prompt(q_p.reshape(nb, seq, -1), k_p.reshape(nb, seq, -1), vt_p, tq=tq)
    pm6 = mods[1, :nb][:, :, None, :]
    tiles_moe = seq // tmoe
    xp, hp, route_p, cnt_p = _oproj_route(
        xp, pm6, pl.BlockSpec((None, 6, 1, d), lambda i: (i // tiles_moe, 0, 0, 0)),
        o_p.reshape(n_p, -1), wo, g2, wrc, br, tm=tmoe)
    yp = _moe_routed(
        hp, xp, pm6, pl.BlockSpec((None, 6, 1, d), lambda i, j, *_: (i // tiles_moe, 0, 0, 0)),
        route_p, cnt_p, weg, weu, wed, fg, tm=tmoe)

    tc_s, ts_s = _rope_tables(past + jnp.arange(t_s))
    tc_s = jnp.tile(tc_s, (ns_b, 1))
    ts_s = jnp.tile(ts_s, (ns_b, 1))
    wqn = wq_nope.reshape(Q_LORA, -1).astype(BF16)
    wqp = wq_pe.reshape(Q_LORA, -1).astype(BF16)
    wqs = wq_sw.reshape(Q_LORA, -1).astype(BF16)
    wukt = w_uk[0].reshape(KV_LORA, MLA_HEADS, QK_NOPE).transpose(1, 2, 0).astype(BF16)
    wuv_h = w_uv[0].reshape(KV_LORA, MLA_HEADS, V_HD).transpose(1, 0, 2).astype(BF16)
    sm = sample_mod(1, 0)
    ckv_s, kpe_s, qlat, qpe = _mla_sample(
        xs, sm, sspec(n_s), g1, (wdc, gq, gkv, wqn, wqp, wqs, wukt), tc_s, ts_s, tm=n_s)
    o_lat = _attn_sample(
        qlat.reshape(ns_b, t_s * MLA_HEADS, KV_LORA), qpe.reshape(ns_b, t_s * MLA_HEADS, QK_ROPE),
        cache_ckv[0], cache_kpe[0], ckv_s.reshape(ns_b, t_s, KV_LORA),
        kpe_s.reshape(ns_b, t_s, QK_ROPE))
    sm6 = jnp.broadcast_to(mods[1, nb:][:, None], (ns_b, t_s, 6, d)).reshape(n_s, 6, d).transpose(1, 0, 2)
    xs, hs, route_s, cnt_s = _oproj_route(
        xs, sm6, pl.BlockSpec((6, n_s, d), lambda i: (0, i, 0)),
        o_lat.reshape(n_s, MLA_HEADS * KV_LORA), wo, g2, wrc, br, wuv_h, tm=n_s)
    ys = _moe_routed(
        hs, xs, sm6, pl.BlockSpec((6, n_s, d), lambda i, j, *_: (0, i, 0)),
        route_s, cnt_s, weg, weu, wed, fg, tm=n_s)

    return (yp.reshape(nb, seq, d), ys.reshape(ns_b, t_s, d),
            conv_p[None], conv_s[None], gv_s.reshape(ns_b, t_s, c_ch)[None],
            ckv_p.reshape(nb, seq, KV_LORA)[None], kpe_p.reshape(nb, seq, QK_ROPE)[None],
            ckv_s.reshape(ns_b, t_s, KV_LORA)[None], kpe_s.reshape(ns_b, t_s, QK_ROPE)[None])
```

```python
import functools
import math

import jax
import jax.numpy as jnp
from jax import lax
from jax.experimental import pallas as pl
from jax.experimental.pallas import tpu as pltpu

F32 = jnp.float32
BF16 = jnp.bfloat16

EPS = 1e-6
CHUNK = 64
CONV_W = 31
HIST = CONV_W - 1
HIST_PAD = 32
GMLP_CHUNK = 128
GMLP_HEADS = 8
MLA_HEADS = 8
Q_LORA = 256
KV_LORA = 256
QK_NOPE = 128
QK_ROPE = 64
V_HD = 128
V_AUG = V_HD + 16
QK_CAT = 256
ROPE_BASE = 10000.0
N_EXPERTS = 8
MOE_ROWS = 256
MOE_PAD = 128
LANES = 128
SUBLANES = 8
CONV_UNROLL = True
NEG_BIG = -1e30

VMEM_LIMIT = 56 * 1024 * 1024


def _cparams(*sem):
    return pltpu.CompilerParams(dimension_semantics=sem, vmem_limit_bytes=VMEM_LIMIT)


def _const_spec(shape):
    nd = len(shape)
    return pl.BlockSpec(shape, lambda *_: (0,) * nd, pipeline_mode=pl.Buffered(1))


def _modulate(x, g, shift, scale):
    y = x * lax.rsqrt(jnp.mean(x * x, axis=-1, keepdims=True) + EPS)
    return (y * g) * (1.0 + scale) + shift


def _layer_norm(x, g, b):
    mu = jnp.mean(x, axis=-1, keepdims=True)
    xc = x - mu
    var = jnp.mean(xc * xc, axis=-1, keepdims=True)
    return xc * lax.rsqrt(var + EPS) * g + b


def _dot(a, b):
    return jnp.dot(a, b, preferred_element_type=F32)


_NT = (((1,), (1,)), ((), ()))


def _adaln_kernel(c_ref, w_ref, b_ref, o_ref):
    c = c_ref[...]
    a = jax.nn.silu(c).astype(BF16)
    o_ref[...] = _dot(a, w_ref[...].astype(BF16)) + b_ref[...]


def _adaln(c_all, w_ada, b_ada, tn=1536):
    depth, d, n = w_ada.shape
    rows = c_all.shape[0]
    return pl.pallas_call(
        _adaln_kernel,
        grid=(depth, n // tn),
        in_specs=[
            pl.BlockSpec((rows, d), lambda l, j: (0, 0)),
            pl.BlockSpec((None, d, tn), lambda l, j: (l, 0, j)),
            pl.BlockSpec((None, 1, tn), lambda l, j: (l, 0, j)),
        ],
        out_specs=pl.BlockSpec((None, rows, tn), lambda l, j: (l, 0, j)),
        out_shape=jax.ShapeDtypeStruct((depth, rows, n), F32),
        compiler_params=_cparams("parallel", "parallel"),
        name="adaln",
    )(c_all, w_ada, b_ada.reshape(depth, 1, n))


def _mixer_kernel(*refs, ns, t, carry_hist, conv_rows):
    if carry_hist:
        (x_ref, mod_ref, g_ref, win_ref, bin_ref, cw_ref, cb_ref, lag_ref, lab_ref,
         lvg_ref, lvb_ref, wsp_ref, bsp_ref, woa_ref, wob_ref, bout_ref,
         xo_ref, conv_ref, aext_ref, ya_ref, yb_ref, tapwin_ref) = refs
        hist_ref = v_ref = None
    else:
        (x_ref, mod_ref, g_ref, win_ref, bin_ref, cw_ref, cb_ref, lag_ref, lab_ref,
         lvg_ref, lvb_ref, wsp_ref, bsp_ref, woa_ref, wob_ref, bout_ref, hist_ref,
         xo_ref, conv_ref, v_ref, aext_ref, ya_ref, yb_ref, tapwin_ref) = refs
    c = cw_ref.shape[1]
    rows = ns * t

    x = x_ref[...]
    h = _modulate(x, g_ref[...], mod_ref[0], mod_ref[1]).astype(BF16)
    p = _dot(h, win_ref[...]) + bin_ref[...]

    a = p[:, :c] * jax.nn.sigmoid(p[:, c:2 * c])
    if carry_hist:
        @pl.when(pl.program_id(1) == 0)
        def _():
            aext_ref[:, :HIST_PAD, :] = jnp.zeros((ns, HIST_PAD, c), F32)
    else:
        aext_ref[:, HIST_PAD - HIST:HIST_PAD, :] = hist_ref[...]
    aext_ref[:, HIST_PAD:, :] = a.reshape(ns, t, c)

    off = HIST_PAD - HIST
    n_cc = t // conv_rows
    lag = lag_ref[...]
    lab = lab_ref[...]
    cb = cb_ref[...]

    def finish(acc):
        y = _layer_norm(acc + cb, lag, lab)
        return jax.nn.silu(y).astype(BF16)

    def conv_direct(s):
        acc = jnp.zeros((conv_rows, c), F32)
        for k in range(CONV_W):
            acc = acc + aext_ref[s, off + k:off + k + conv_rows, :] * cw_ref[k:k + 1, :]
        return finish(acc)

    def conv_phased(s, r0):
        acc = jnp.zeros((conv_rows, c), F32)
        for k in range(CONV_W):
            r = (off + k) % SUBLANES
            q8 = pl.multiple_of(r0 + (off + k - r), SUBLANES)
            if r == 0:
                tap = aext_ref[s, pl.ds(q8, conv_rows), :]
            else:
                tap = tapwin_ref[r - 1, pl.ds(q8, conv_rows), :]
            acc = acc + tap * cw_ref[k:k + 1, :]
        return finish(acc)

    for s in range(ns):
        if n_cc == 1:
            ya_ref[s * t:(s + 1) * t, :] = conv_direct(s)
        else:
            span = t + HIST_PAD - SUBLANES
            for r in range(1, SUBLANES):
                tapwin_ref[r - 1] = aext_ref[s, r:r + span, :]

            def body(i, carry, s=s):
                r0 = pl.multiple_of(i * conv_rows, conv_rows)
                ya_ref[pl.ds(s * t + r0, conv_rows), :] = conv_phased(s, r0)
                return carry
            lax.fori_loop(0, n_cc, body, 0, unroll=CONV_UNROLL)

    conv_ref[...] = aext_ref[:, t + HIST_PAD - HIST:t + HIST_PAD, :]
    if carry_hist:
        aext_ref[:, :HIST_PAD, :] = aext_ref[:, t:t + HIST_PAD, :]

    z = jax.nn.gelu(p[:, 2 * c:])
    u = z[:, :c]
    v = _layer_norm(z[:, c:], lvg_ref[...], lvb_ref[...])
    if v_ref is not None:
        v_ref[...] = v
    ri = lax.broadcasted_iota(jnp.int32, (GMLP_CHUNK, GMLP_CHUNK), 0)
    ci = lax.broadcasted_iota(jnp.int32, (GMLP_CHUNK, GMLP_CHUNK), 1)
    tril = ri >= ci
    low_half = ci < (LANES // 2)
    w_heads = [jnp.where(tril, wsp_ref[hh], 0.0).astype(BF16) for hh in range(GMLP_HEADS)]
    for blk in range(rows // GMLP_CHUNK):
        r = slice(blk * GMLP_CHUNK, (blk + 1) * GMLP_CHUNK)
        for j in range(c // LANES):
            l = slice(j * LANES, (j + 1) * LANES)
            vp = v[r, l]
            lo = jnp.where(low_half, vp, 0.0).astype(BF16)
            hi = jnp.where(low_half, 0.0, vp).astype(BF16)
            sp = _dot(w_heads[2 * j], lo) + _dot(w_heads[2 * j + 1], hi)
            yb_ref[r, l] = (u[r, l] * (sp + bsp_ref[:, l])).astype(BF16)

    out = _dot(ya_ref[...], woa_ref[...]) + _dot(yb_ref[...], wob_ref[...]) + bout_ref[...]
    xo_ref[...] = x + mod_ref[2] * out


def _mixer(x, mod, mod_spec, g1, wts, *, ns, t, n_outer, n_inner, hist=None):
    (w_in, b_in, cw, cb, lag, lab, lvg, lvb, wsp, bsp, woa, wob, b_out) = wts
    d = x.shape[1]
    c = cw.shape[1]
    rows = ns * t
    carry_hist = hist is None
    tile = lambda o, i: (o * n_inner + i, 0)
    in_specs = [
        pl.BlockSpec((rows, d), tile),
        mod_spec,
        _const_spec((1, d)),
        _const_spec(w_in.shape), _const_spec(b_in.shape),
        _const_spec(cw.shape), _const_spec(cb.shape),
        _const_spec(lag.shape), _const_spec(lab.shape),
        _const_spec(lvg.shape), _const_spec(lvb.shape),
        _const_spec(wsp.shape), _const_spec(bsp.shape),
        _const_spec(woa.shape), _const_spec(wob.shape), _const_spec(b_out.shape),
    ]
    args = [x, mod, g1, w_in, b_in, cw, cb, lag, lab, lvg, lvb, wsp, bsp, woa, wob, b_out]
    n_tiles = n_outer * n_inner
    out_shape = [jax.ShapeDtypeStruct(x.shape, F32)]
    out_specs = [pl.BlockSpec((rows, d), tile)]
    if carry_hist:
        out_shape.append(jax.ShapeDtypeStruct((n_outer * ns, HIST, c), F32))
        out_specs.append(pl.BlockSpec((ns, HIST, c), lambda o, i: (o, 0, 0)))
    else:
        in_specs.append(pl.BlockSpec((ns, HIST, c), lambda o, i: (o * n_inner + i, 0, 0)))
        args.append(hist)
        out_shape.append(jax.ShapeDtypeStruct((n_tiles * ns, HIST, c), F32))
        out_specs.append(pl.BlockSpec((ns, HIST, c), lambda o, i: (o * n_inner + i, 0, 0)))
        out_shape.append(jax.ShapeDtypeStruct((x.shape[0], c), F32))
        out_specs.append(pl.BlockSpec((rows, c), tile))
    conv_rows = min(t, 64)
    return pl.pallas_call(
        functools.partial(_mixer_kernel, ns=ns, t=t, carry_hist=carry_hist, conv_rows=conv_rows),
        grid=(n_outer, n_inner),
        in_specs=in_specs,
        out_specs=out_specs,
        out_shape=out_shape,
        scratch_shapes=[
            pltpu.VMEM((ns, t + HIST_PAD, c), F32),
            pltpu.VMEM((rows, c), BF16),
            pltpu.VMEM((rows, c), BF16),
            pltpu.VMEM((SUBLANES - 1, t + HIST_PAD - SUBLANES, c), F32),
        ],
        compiler_params=_cparams("arbitrary", "arbitrary"),
        name="mixer_prompt" if carry_hist else "mixer_sample",
    )(*args)


def _ffn_kernel(x_ref, mod_ref, g_ref, wgu_ref, wd_ref, o_ref, *, chunk):
    x = x_ref[...]
    h = _modulate(x, g_ref[...], mod_ref[0], mod_ref[1]).astype(BF16)
    ff = wd_ref.shape[0]
    acc = None
    for c0 in range(0, ff, chunk):
        c1 = min(c0 + chunk, ff)
        gate = _dot(h, wgu_ref[:, c0:c1])
        up = _dot(h, wgu_ref[:, ff + c0:ff + c1])
        act = (jax.nn.silu(gate) * up).astype(BF16)
        part = _dot(act, wd_ref[c0:c1, :])
        acc = part if acc is None else acc + part
    o_ref[...] = x + mod_ref[2] * acc


def _ffn(x, mod, mod_spec, g2, wgu, wd, *, tm, chunk):
    n, d = x.shape
    return pl.pallas_call(
        functools.partial(_ffn_kernel, chunk=chunk),
        grid=(n // tm,),
        in_specs=[
            pl.BlockSpec((tm, d), lambda i: (i, 0)),
            mod_spec,
            _const_spec((1, d)),
            _const_spec(wgu.shape), _const_spec(wd.shape),
        ],
        out_specs=pl.BlockSpec((tm, d), lambda i: (i, 0)),
        out_shape=jax.ShapeDtypeStruct((n, d), F32),
        compiler_params=_cparams("parallel"),
        name="ffn",
    )(x, mod, g2, wgu, wd)


def _mla_latents(x_ref, mod_ref, g_ref, wdc_ref, gq_ref, gkv_ref, tc, ts):
    x = x_ref[...]
    h = _modulate(x, g_ref[...], mod_ref[0], mod_ref[1]).astype(BF16)
    p = _dot(h, wdc_ref[...])
    cq = p[:, :Q_LORA]
    cq = cq * lax.rsqrt(jnp.mean(cq * cq, axis=-1, keepdims=True) + EPS) * gq_ref[...]
    ckv = p[:, Q_LORA:Q_LORA + KV_LORA]
    ckv = ckv * lax.rsqrt(jnp.mean(ckv * ckv, axis=-1, keepdims=True) + EPS) * gkv_ref[...]
    b0 = Q_LORA + KV_LORA
    kpe = p[:, b0:b0 + LANES] * tc + p[:, b0 + LANES:b0 + 2 * LANES] * ts
    return cq.astype(BF16), ckv, kpe


def _mla_prompt_kernel(x_ref, mod_ref, g_ref, wdc_ref, gq_ref, gkv_ref, tc_ref, ts_ref,
                       wqa_ref, wqb_ref, wuk_ref, wuvt_ref,
                       ckv_ref, kpe_ref, q_ref, k_ref, vt_ref):
    tc = tc_ref[...]
    ts = ts_ref[...]
    cq, ckv, kpe = _mla_latents(x_ref, mod_ref, g_ref, wdc_ref, gq_ref, gkv_ref, tc, ts)
    ckv_ref[...] = ckv
    kpe_ref[...] = kpe[:, :QK_ROPE]
    qa = _dot(cq, wqa_ref[...])
    qb = _dot(cq, wqb_ref[...])
    ckv_b = ckv.astype(BF16)
    kn = _dot(ckv_b, wuk_ref[...])
    vt = lax.dot_general(wuvt_ref[...], ckv_b, _NT, preferred_element_type=F32).astype(BF16)
    tm = vt.shape[1]
    pad_rows = V_AUG - V_HD
    ones_row = jnp.where(lax.broadcasted_iota(jnp.int32, (pad_rows, tm), 0) == 0, 1.0, 0.0).astype(BF16)
    for hh in range(MLA_HEADS):
        vt_ref[hh * V_AUG:hh * V_AUG + V_HD, :] = vt[hh * V_HD:(hh + 1) * V_HD, :]
        vt_ref[hh * V_AUG + V_HD:(hh + 1) * V_AUG, :] = ones_row
    kpe_b = kpe.astype(BF16)
    for hh in range(MLA_HEADS):
        base = hh * QK_CAT
        q_ref[:, base:base + QK_NOPE] = qa[:, base:base + QK_NOPE].astype(BF16)
        q_ref[:, base + QK_NOPE:base + QK_CAT] = (
            qa[:, base + QK_NOPE:base + QK_CAT] * tc
            + qb[:, hh * LANES:(hh + 1) * LANES] * ts).astype(BF16)
        k_ref[:, base:base + QK_NOPE] = kn[:, hh * QK_NOPE:(hh + 1) * QK_NOPE].astype(BF16)
        k_ref[:, base + QK_NOPE:base + QK_CAT] = kpe_b


def _mla_prompt(x, mod, mod_spec, g1, wts, tc, ts, *, tm, tiles_per_seq):
    wdc, gq, gkv, wqa, wqb, wuk, wuvt = wts
    n, d = x.shape
    hq = MLA_HEADS * QK_CAT
    hv = MLA_HEADS * V_AUG
    nb = n // (tm * tiles_per_seq)
    row = lambda i: (i, 0)
    pos = lambda i: (i % tiles_per_seq, 0)
    return pl.pallas_call(
        _mla_prompt_kernel,
        grid=(n // tm,),
        in_specs=[
            pl.BlockSpec((tm, d), row), mod_spec, _const_spec((1, d)),
            _const_spec(wdc.shape), _const_spec(gq.shape), _const_spec(gkv.shape),
            pl.BlockSpec((tm, LANES), pos), pl.BlockSpec((tm, LANES), pos),
            _const_spec(wqa.shape), _const_spec(wqb.shape),
            _const_spec(wuk.shape), _const_spec(wuvt.shape),
        ],
        out_specs=[
            pl.BlockSpec((tm, KV_LORA), row), pl.BlockSpec((tm, QK_ROPE), row),
            pl.BlockSpec((tm, hq), row), pl.BlockSpec((tm, hq), row),
            pl.BlockSpec((None, hv, tm), lambda i: (i // tiles_per_seq, 0, i % tiles_per_seq)),
        ],
        out_shape=[
            jax.ShapeDtypeStruct((n, KV_LORA), F32), jax.ShapeDtypeStruct((n, QK_ROPE), F32),
            jax.ShapeDtypeStruct((n, hq), BF16), jax.ShapeDtypeStruct((n, hq), BF16),
            jax.ShapeDtypeStruct((nb, hv, tm * tiles_per_seq), BF16),
        ],
        compiler_params=_cparams("parallel"),
        name="mla_proj_prompt",
    )(x, mod, g1, wdc, gq, gkv, tc, ts, wqa, wqb, wuk, wuvt)


def _mla_sample_kernel(x_ref, mod_ref, g_ref, wdc_ref, gq_ref, gkv_ref, tc_ref, ts_ref,
                       wqn_ref, wqp_ref, wqs_ref, wukt_ref,
                       ckv_ref, kpe_ref, qlat_ref, qpe_ref):
    tc = tc_ref[...]
    ts = ts_ref[...]
    cq, ckv, kpe = _mla_latents(x_ref, mod_ref, g_ref, wdc_ref, gq_ref, gkv_ref, tc, ts)
    ckv_ref[...] = ckv
    kpe_ref[...] = kpe[:, :QK_ROPE]
    qn = _dot(cq, wqn_ref[...]).astype(BF16)
    qp = _dot(cq, wqp_ref[...])
    qs = _dot(cq, wqs_ref[...])
    for j in range(MLA_HEADS * QK_ROPE // LANES):
        l = slice(j * LANES, (j + 1) * LANES)
        qpe_ref[:, l] = (qp[:, l] * tc + qs[:, l] * ts).astype(BF16)
    for hh in range(MLA_HEADS):
        qlat_ref[:, hh * KV_LORA:(hh + 1) * KV_LORA] = _dot(
            qn[:, hh * QK_NOPE:(hh + 1) * QK_NOPE], wukt_ref[hh]).astype(BF16)


def _mla_sample(x, mod, mod_spec, g1, wts, tc, ts, *, tm):
    wdc, gq, gkv, wqn, wqp, wqs, wukt = wts
    n, d = x.shape
    row = lambda i: (i, 0)
    return pl.pallas_call(
        _mla_sample_kernel,
        grid=(n // tm,),
        in_specs=[
            pl.BlockSpec((tm, d), row), mod_spec, _const_spec((1, d)),
            _const_spec(wdc.shape), _const_spec(gq.shape), _const_spec(gkv.shape),
            pl.BlockSpec((tm, LANES), row), pl.BlockSpec((tm, LANES), row),
            _const_spec(wqn.shape), _const_spec(wqp.shape), _const_spec(wqs.shape),
            _const_spec(wukt.shape),
        ],
        out_specs=[
            pl.BlockSpec((tm, KV_LORA), row), pl.BlockSpec((tm, QK_ROPE), row),
            pl.BlockSpec((tm, MLA_HEADS * KV_LORA), row),
            pl.BlockSpec((tm, MLA_HEADS * QK_ROPE), row),
        ],
        out_shape=[
            jax.ShapeDtypeStruct((n, KV_LORA), F32), jax.ShapeDtypeStruct((n, QK_ROPE), F32),
            jax.ShapeDtypeStruct((n, MLA_HEADS * KV_LORA), BF16),
            jax.ShapeDtypeStruct((n, MLA_HEADS * QK_ROPE), BF16),
        ],
        compiler_params=_cparams("parallel"),
        name="mla_proj_sample",
    )(x, mod, g1, wdc, gq, gkv, tc, ts, wqn, wqp, wqs, wukt)


def _attn_prompt_kernel(q_ref, k_ref, vt_ref, o_ref, s_scr, p_scr, acc_scr, *, tq, tk):
    qi = pl.program_id(2)
    n_full = 2 * qi

    def qk_to(slot, t):
        k = k_ref[pl.ds(pl.multiple_of(t * tk, tk), tk), :]
        s = lax.dot_general(k, q_ref[...], _NT, preferred_element_type=F32)
        s_scr[slot] = s
        return jnp.max(s, axis=0, keepdims=True)

    def softmax_from(slot, m, bm):
        m_new = jnp.maximum(m, bm)
        alpha = jnp.exp2(m - m_new)
        p_scr[slot] = jnp.exp2(s_scr[slot] - m_new).astype(BF16)
        return m_new, alpha

    def pv_from(slot, t, alpha):
        vt = vt_ref[:, pl.ds(pl.multiple_of(t * tk, tk), tk)]
        acc_scr[...] = alpha * acc_scr[...] + _dot(vt, p_scr[slot])

    p_scr[1] = jnp.zeros((tk, tq), BF16)
    acc_scr[...] = jnp.zeros((V_AUG, tq), F32)
    m0 = jnp.full((1, tq), NEG_BIG, F32)
    bm0 = qk_to(0, 0)

    def body(i, carry):
        m, bm, a_prev = carry
        t = 2 * i
        bm1 = qk_to(1, t + 1)
        m, a0 = softmax_from(0, m, bm)
        pv_from(1, jnp.maximum(t - 1, 0), a_prev)
        bm2 = qk_to(0, t + 2)
        m, a1 = softmax_from(1, m, bm1)
        pv_from(0, t, a0)
        return m, bm2, a1

    m, bm, a_prev = lax.fori_loop(0, qi, body, (m0, bm0, jnp.ones((1, tq), F32)))
    pv_from(1, jnp.maximum(n_full - 1, 0), a_prev)
    kk = lax.broadcasted_iota(jnp.int32, (tk, tk), 0) // CHUNK
    qq = lax.broadcasted_iota(jnp.int32, (tk, tk), 1) // CHUNK
    vis = qq >= kk
    s_a = jnp.where(vis, s_scr[0, :, :tk], NEG_BIG)
    bm_d = jnp.concatenate([jnp.max(s_a, axis=0, keepdims=True), bm[:, tk:]], axis=1)
    m_new = jnp.maximum(m, bm_d)
    alpha = jnp.exp2(m - m_new)
    p_a = jnp.exp2(s_a - m_new[:, :tk])
    p_b = jnp.exp2(s_scr[0, :, tk:] - m_new[:, tk:])
    p_scr[0, :, :tk] = p_a.astype(BF16)
    p_scr[0, :, tk:] = p_b.astype(BF16)
    pv_from(0, n_full, alpha)
    k1 = k_ref[pl.ds(pl.multiple_of((n_full + 1) * tk, tk), tk), :]
    s1 = lax.dot_general(k1, q_ref[tk:, :], _NT, preferred_element_type=F32)
    s1 = jnp.where(vis, s1, NEG_BIG)
    m_b = m_new[:, tk:]
    m_b2 = jnp.maximum(m_b, jnp.max(s1, axis=0, keepdims=True))
    a1 = jnp.exp2(m_b - m_b2)
    p1 = jnp.exp2(s1 - m_b2)
    vt1 = vt_ref[:, pl.ds(pl.multiple_of((n_full + 1) * tk, tk), tk)]
    acc_b = a1 * acc_scr[:, tk:] + _dot(vt1, p1.astype(BF16))
    o_ref[:tk, :] = (acc_scr[:V_HD, :tk] / acc_scr[V_HD:V_HD + 1, :tk]).T.astype(o_ref.dtype)
    o_ref[tk:, :] = (acc_b[:V_HD] / acc_b[V_HD:V_HD + 1]).T.astype(o_ref.dtype)


def _attn_prompt(q, k, vt, *, tq):
    b, s, _ = q.shape
    tk = tq // 2
    return pl.pallas_call(
        functools.partial(_attn_prompt_kernel, tq=tq, tk=tk),
        grid=(b, MLA_HEADS, s // tq),
        in_specs=[
            pl.BlockSpec((None, tq, QK_CAT), lambda bi, hi, qi: (bi, qi, hi)),
            pl.BlockSpec((None, s, QK_CAT), lambda bi, hi, qi: (bi, 0, hi)),
            pl.BlockSpec((None, V_AUG, s), lambda bi, hi, qi: (bi, hi, 0)),
        ],
        out_specs=pl.BlockSpec((None, tq, V_HD), lambda bi, hi, qi: (bi, qi, hi)),
        out_shape=jax.ShapeDtypeStruct((b, s, MLA_HEADS * V_HD), BF16),
        scratch_shapes=[pltpu.VMEM((2, tk, tq), F32), pltpu.VMEM((2, tk, tq), BF16),
                        pltpu.VMEM((V_AUG, tq), F32)],
        compiler_params=_cparams("parallel", "parallel", "arbitrary"),
        name="attn_prompt",
    )(q, k, vt)


def _attn_sample_kernel(ql_ref, qp_ref, cc_ref, cpt_ref, nc_ref, np_ref, o_ref):
    for g in range(ql_ref.shape[0]):
        ql = ql_ref[g]
        qp = qp_ref[g]
        cc = cc_ref[g].astype(BF16)
        cpt = cpt_ref[g].astype(BF16)
        nc = nc_ref[g].astype(BF16)
        npe = np_ref[g].astype(BF16)
        s_c = lax.dot_general(ql, cc, _NT, preferred_element_type=F32) + _dot(qp, cpt)
        s_n = (lax.dot_general(ql, nc, _NT, preferred_element_type=F32)
               + lax.dot_general(qp, npe, _NT, preferred_element_type=F32))
        m = jnp.maximum(jnp.max(s_c, axis=-1, keepdims=True), jnp.max(s_n, axis=-1, keepdims=True))
        p_c = jnp.exp2(s_c - m)
        p_n = jnp.exp2(s_n - m)
        l = jnp.sum(p_c, axis=-1, keepdims=True) + jnp.sum(p_n, axis=-1, keepdims=True)
        o = _dot(p_c.astype(BF16), cc) + _dot(p_n.astype(BF16), nc)
        o_ref[g] = (o / l).astype(o_ref.dtype)


def _attn_sample(qlat, qpe, cache_ckv, cache_kpe_t, ckv_new, kpe_new, *, streams=2):
    nb, past, _ = cache_ckv.shape
    r = qlat.shape[1]
    t = ckv_new.shape[1]
    blk = lambda shape: pl.BlockSpec((streams,) + shape, lambda i: (i, 0, 0))
    return pl.pallas_call(
        _attn_sample_kernel,
        grid=(nb // streams,),
        in_specs=[blk((r, KV_LORA)), blk((r, QK_ROPE)), blk((past, KV_LORA)),
                  blk((QK_ROPE, past)), blk((t, KV_LORA)), blk((t, QK_ROPE))],
        out_specs=blk((r, KV_LORA)),
        out_shape=jax.ShapeDtypeStruct((nb, r, KV_LORA), BF16),
        compiler_params=_cparams("parallel"),
        name="attn_sample",
    )(qlat, qpe, cache_ckv, cache_kpe_t, ckv_new, kpe_new)


def _route_tail(xm, mod_ref, g_ref, wrc_ref, br_ref, xo_ref, h_ref, route_ref, cnt_ref):
    tm = xm.shape[0]
    xo_ref[...] = xm
    hf = _modulate(xm, g_ref[...], mod_ref[3], mod_ref[4])
    hb = hf.astype(BF16)
    h_ref[...] = hb
    h_lo = (hf - hb.astype(F32)).astype(BF16)
    both = _dot(hb, wrc_ref[...])
    logits = (both[:, :LANES] + both[:, LANES:] + _dot(h_lo, wrc_ref[:, :LANES])) + br_ref[...]
    lane = lax.broadcasted_iota(jnp.int32, (tm, LANES), 1)
    m1 = jnp.max(logits, axis=-1, keepdims=True)
    i1 = jnp.min(jnp.where(logits == m1, lane, LANES), axis=-1, keepdims=True)
    rest = jnp.where(lane == i1, -jnp.inf, logits)
    m2 = jnp.max(rest, axis=-1, keepdims=True)
    i2 = jnp.min(jnp.where(rest == m2, lane, LANES), axis=-1, keepdims=True)
    e2 = jnp.exp(m2 - m1)
    den = 1.0 + e2
    sel1 = lane == i1
    sel2 = lane == i2
    onehot = jnp.where(sel1, 1.0, jnp.where(sel2, 1.0, 0.0))
    ri = lax.broadcasted_iota(jnp.int32, (tm, tm), 0)
    ci = lax.broadcasted_iota(jnp.int32, (tm, tm), 1)
    before = jnp.where(ri > ci, 1.0, 0.0).astype(BF16)
    rank = _dot(before, onehot.astype(BF16))
    cnt = jnp.sum(onehot, axis=0, keepdims=True)
    nblk = jnp.floor((cnt + (MOE_PAD - 1)) * (1.0 / MOE_PAD))
    r8 = lax.broadcasted_iota(jnp.int32, (LANES, LANES), 0)
    c8 = lax.broadcasted_iota(jnp.int32, (LANES, LANES), 1)
    upper = jnp.where(r8 < c8, 1.0, 0.0).astype(BF16)
    first_blk = _dot(jnp.broadcast_to(nblk, (8, LANES)).astype(BF16), upper)[0:1]
    slot = first_blk * MOE_PAD + rank
    d1 = jnp.sum(jnp.where(sel1, slot, 0.0), axis=-1, keepdims=True)
    d2 = jnp.sum(jnp.where(sel2, slot, 0.0), axis=-1, keepdims=True)
    route_ref[...] = jnp.where(lane == 0, d1, jnp.where(lane == 1, d2, jnp.where(
        lane == 2, 1.0 / den, jnp.where(lane == 3, e2 / den, 0.0))))
    cnt_ref[...] = jnp.broadcast_to(cnt, (8, LANES))


def _oproj_route_kernel(x_ref, mod_ref, o_ref, wo_ref, g_ref, wrc_ref, br_ref,
                        xo_ref, h_ref, route_ref, cnt_ref):
    xm = x_ref[...] + mod_ref[2] * _dot(o_ref[...], wo_ref[...])
    _route_tail(xm, mod_ref, g_ref, wrc_ref, br_ref, xo_ref, h_ref, route_ref, cnt_ref)


def _oproj_latent_route_kernel(x_ref, mod_ref, o_ref, wuv_ref, wo_ref, g_ref, wrc_ref,
                               br_ref, xo_ref, h_ref, route_ref, cnt_ref):
    acc = None
    for hh in range(MLA_HEADS):
        oh = _dot(o_ref[:, hh * KV_LORA:(hh + 1) * KV_LORA], wuv_ref[hh]).astype(BF16)
        part = _dot(oh, wo_ref[hh * V_HD:(hh + 1) * V_HD, :])
        acc = part if acc is None else acc + part
    xm = x_ref[...] + mod_ref[2] * acc
    _route_tail(xm, mod_ref, g_ref, wrc_ref, br_ref, xo_ref, h_ref, route_ref, cnt_ref)


def _oproj_route(x, mod, mod_spec, o, wo, g2, wrc, br, wuv=None, *, tm):
    n, d = x.shape
    row = lambda i: (i, 0)
    in_specs = [pl.BlockSpec((tm, d), row), mod_spec, pl.BlockSpec((tm, o.shape[1]), row)]
    args = [x, mod, o]
    if wuv is not None:
        in_specs.append(_const_spec(wuv.shape))
        args.append(wuv)
    in_specs += [_const_spec(wo.shape), _const_spec((1, d)), _const_spec(wrc.shape),
                 _const_spec(br.shape)]
    args += [wo, g2, wrc, br]
    return pl.pallas_call(
        _oproj_route_kernel if wuv is None else _oproj_latent_route_kernel,
        grid=(n // tm,),
        in_specs=in_specs,
        out_specs=[pl.BlockSpec((tm, d), row), pl.BlockSpec((tm, d), row),
                   pl.BlockSpec((tm, LANES), row), pl.BlockSpec((None, 8, LANES), lambda i: (i, 0, 0))],
        out_shape=[jax.ShapeDtypeStruct((n, d), F32), jax.ShapeDtypeStruct((n, d), BF16),
                   jax.ShapeDtypeStruct((n, LANES), F32),
                   jax.ShapeDtypeStruct((n // tm, 8, LANES), F32)],
        compiler_params=_cparams("parallel"),
        name="oproj_route" if wuv is None else "oproj_latent_route",
    )(*args)


def _moe_routed_kernel(row0_ref, npad_ref, h_ref, x_ref, mod_ref, rc_ref, rr_ref,
                       wgu_ref, wd_ref, fg_ref, y_ref, acc_ref):
    i = pl.program_id(0)
    e = pl.program_id(1)
    n_exp = pl.num_programs(1)
    tm = h_ref.shape[0]
    start = row0_ref[i * n_exp + e]
    n_pad = npad_ref[i * n_exp + e]
    per = MOE_ROWS // MOE_PAD
    has_head = n_pad % per

    @pl.when(e == 0)
    def _():
        acc_ref[...] = jnp.zeros(acc_ref.shape, F32)

    def block(rows, first_slot):
        base = first_slot.astype(F32)
        rr = rr_ref[...]
        srow = lax.broadcasted_iota(jnp.int32, (rows, tm), 0).astype(F32) + base
        hit1 = rr[0:1, :] == srow
        hit2 = rr[1:2, :] == srow
        sel = jnp.where(hit1, 1.0, jnp.where(hit2, 1.0, 0.0)).astype(BF16)
        gate = jnp.sum(jnp.where(hit1, rr[2:3, :], jnp.where(hit2, rr[3:4, :], 0.0)),
                       axis=-1, keepdims=True)
        hb = _dot(sel, h_ref[...]).astype(BF16)
        gu = _dot(hb, wgu_ref[...])
        ff = gu.shape[1] // 2
        act = (jax.nn.silu(gu[:, :ff]) * gu[:, ff:]).astype(BF16)
        yb = (_dot(act, wd_ref[...]) * gate).astype(BF16)
        rc = rc_ref[...]
        scol = lax.broadcasted_iota(jnp.int32, (tm, rows), 1).astype(F32) + base
        sel_t = jnp.where(rc[:, 0:1] == scol, 1.0,
                          jnp.where(rc[:, 1:2] == scol, 1.0, 0.0)).astype(BF16)
        acc_ref[...] += _dot(sel_t, yb)

    for units in range(1, per):
        @pl.when(has_head == units)
        def _(units=units):
            block(units * MOE_PAD, start)

    def body(b, carry):
        block(MOE_ROWS, start + has_head * MOE_PAD + b * MOE_ROWS)
        return carry

    lax.fori_loop(0, n_pad // per, body, 0)

    @pl.when(e == n_exp - 1)
    def _():
        xn = x_ref[...] + mod_ref[5] * acc_ref[...]
        y_ref[...] = xn * lax.rsqrt(jnp.mean(xn * xn, axis=-1, keepdims=True) + EPS) * fg_ref[...]


def _moe_routed(h, x, mod, mod_spec2, route, cnt, wgu, wd, fg, *, tm):
    n, d = x.shape
    ne, ff, _ = wd.shape
    n_tiles = n // tm
    n_pad = jnp.ceil(cnt[:, 0, :ne] / MOE_PAD).astype(jnp.int32)
    row0 = (jnp.cumsum(n_pad, axis=1) - n_pad) * MOE_PAD
    route_row = route[:, :8].reshape(n_tiles, tm, 8).transpose(0, 2, 1)
    wspec = lambda shape: pl.BlockSpec((None,) + shape, lambda i, e, r0, npd: (e, 0, 0))
    once = lambda shape: pl.BlockSpec(shape, lambda i, e, r0, npd: (i, 0), pipeline_mode=pl.Buffered(1))
    return pl.pallas_call(
        _moe_routed_kernel,
        grid_spec=pltpu.PrefetchScalarGridSpec(
            num_scalar_prefetch=2,
            grid=(n_tiles, ne),
            in_specs=[
                pl.BlockSpec((tm, d), lambda i, e, r0, npd: (i, 0)), once((tm, d)), mod_spec2,
                pl.BlockSpec((tm, LANES), lambda i, e, r0, npd: (i, 0)),
                pl.BlockSpec((None, 8, tm), lambda i, e, r0, npd: (i, 0, 0)),
                wspec((d, 2 * ff)), wspec((ff, d)),
                pl.BlockSpec((1, d), lambda i, e, r0, npd: (0, 0)),
            ],
            out_specs=pl.BlockSpec((tm, d), lambda i, e, r0, npd: (i, 0)),
            scratch_shapes=[pltpu.VMEM((tm, d), F32)],
        ),
        out_shape=jax.ShapeDtypeStruct((n, d), F32),
        compiler_params=_cparams("parallel", "arbitrary"),
        name="moe_routed",
    )(row0.reshape(-1), n_pad.reshape(-1), h, x, mod, route, route_row, wgu, wd, fg)


def _rope_tables(pos):
    half = QK_ROPE // 2
    inv = jnp.exp(-math.log(ROPE_BASE) * jnp.arange(half, dtype=F32) / half)
    ang = pos.astype(F32)[:, None] * inv[None, :]
    cos, sin = jnp.cos(ang), jnp.sin(ang)
    reps = LANES // QK_ROPE
    tc = jnp.tile(jnp.concatenate([cos, cos], axis=-1), (1, reps))
    ts = jnp.tile(jnp.concatenate([-sin, sin], axis=-1), (1, reps))
    return tc, ts


def _swap_halves(w):
    half = w.shape[-1] // 2
    return jnp.concatenate([w[..., half:], w[..., :half]], axis=-1)


def kernel(x_prompt, x_sample, c_prompt, c_sample, state_conv, cache_ckv, cache_kpe, norm1_g, norm2_g, w_ada, b_ada, w_in_ab, b_in_ab, conv_w, conv_b, ln_conv_g, ln_conv_b, ln_v_g, ln_v_b, w_spatial, b_spatial, w_out_ab, b_out_ab, w_ffn_gu, w_ffn_down, w_dc, g_q, g_kv, w_uq, w_uk, w_uv, w_o, w_router, b_router, w_exp_gu, w_exp_down, final_g):
    nb, seq, d = x_prompt.shape
    ns_b, t_s, _ = x_sample.shape
    past = cache_ckv.shape[2]
    c_ch = conv_w.shape[-1]
    n_p = nb * seq
    n_s = ns_b * t_s
    tm = 1024
    tq = 2048
    tmoe = 1024
    tm_mla = 1024
    tiles_per_seq = seq // tm
    s_group = GMLP_CHUNK // t_s

    xp = x_prompt.reshape(n_p, d)
    xs = x_sample.reshape(n_s, d)

    mods = _adaln(jnp.concatenate([c_prompt, c_sample], axis=0), w_ada, b_ada)
    depth = w_ada.shape[0]
    mods = mods.reshape(depth, nb + ns_b, 6, d)

    def prompt_mod(l, k0):
        return mods[l, :nb, k0:k0 + 3][:, :, None, :]

    def sample_mod(l, k0):
        m = mods[l, nb:, k0:k0 + 3]
        m = jnp.broadcast_to(m[:, None], (ns_b, t_s, 3, d)).reshape(n_s, 3, d)
        return m.transpose(1, 0, 2)

    def pspec(tiles_per_b):
        return pl.BlockSpec((None, 3, 1, d), lambda i: (i // tiles_per_b, 0, 0, 0))

    def sspec(rows):
        return pl.BlockSpec((3, rows, d), lambda i: (0, i, 0))

    g1 = norm1_g[0][None]
    g2 = norm2_g[0][None]
    w_in = w_in_ab[0].astype(BF16)
    woa = w_out_ab[0][:c_ch].astype(BF16)
    wob = w_out_ab[0][c_ch:].astype(BF16)
    hd = c_ch // GMLP_HEADS
    bsp_p = jnp.repeat(b_spatial[0].T, hd, axis=1)
    wsp_p = w_spatial[0]
    eye = jnp.eye(s_group, dtype=F32)
    wsp_s = jnp.einsum('ab,hts->hatbs', eye, w_spatial[0][:, :t_s, :t_s]).reshape(
        GMLP_HEADS, GMLP_CHUNK, GMLP_CHUNK)
    bsp_s = jnp.tile(jnp.repeat(b_spatial[0][:, :t_s].T, hd, axis=1), (s_group, 1))
    common = (w_in, b_in_ab[0][None], conv_w[0], conv_b[0][None], ln_conv_g[0][None],
              ln_conv_b[0][None], ln_v_g[0][None], ln_v_b[0][None])
    tail = (woa, wob, b_out_ab[0][None])

    pm = prompt_mod(0, 0)
    xp, conv_p = _mixer(
        xp, pm, pl.BlockSpec((None, 3, 1, d), lambda o, i: (o, 0, 0, 0)), g1,
        common + (wsp_p, bsp_p) + tail, ns=1, t=tm, n_outer=nb, n_inner=tiles_per_seq)
    sm = sample_mod(0, 0)
    n_st = n_s // GMLP_CHUNK
    xs, conv_s, gv_s = _mixer(
        xs, sm, pl.BlockSpec((3, GMLP_CHUNK, d), lambda o, i: (0, o * n_st + i, 0)), g1,
        common + (wsp_s, bsp_s) + tail, ns=s_group, t=t_s, n_outer=1, n_inner=n_st,
        hist=state_conv[0])

    ff = w_ffn_gu.shape[-1] // 2
    wgu = w_ffn_gu[0].astype(BF16)
    wd = w_ffn_down[0].astype(BF16)
    tm_ffn = 1024
    xp = _ffn(xp, prompt_mod(0, 3), pspec(seq // tm_ffn), g2, wgu, wd, tm=tm_ffn, chunk=1024)
    xs = _ffn(xs, sample_mod(0, 3), sspec(n_s), g2, wgu, wd, tm=n_s, chunk=ff // 2)

    g1 = norm1_g[1][None]
    g2 = norm2_g[1][None]
    scale = (QK_NOPE + QK_ROPE) ** -0.5 * math.log2(math.e)
    zeros64 = jnp.zeros((d, QK_ROPE), F32)
    w_kpe = w_dc[0][:, Q_LORA + KV_LORA:]
    wdc = jnp.concatenate([w_dc[0][:, :Q_LORA + KV_LORA], w_kpe, zeros64,
                           _swap_halves(w_kpe), zeros64], axis=1).astype(BF16)
    wq3 = w_uq[0].reshape(Q_LORA, MLA_HEADS, QK_NOPE + QK_ROPE) * scale
    wq_nope = wq3[:, :, :QK_NOPE]
    wq_pe = wq3[:, :, QK_NOPE:]
    wq_sw = _swap_halves(wq_pe)
    zpad = jnp.zeros((Q_LORA, MLA_HEADS, QK_ROPE), F32)
    wqa = jnp.concatenate([wq_nope, wq_pe, zpad], axis=-1).reshape(Q_LORA, -1).astype(BF16)
    wqb = jnp.concatenate([wq_sw, zpad], axis=-1).reshape(Q_LORA, -1).astype(BF16)
    wuk = w_uk[0].astype(BF16)
    wuvt = w_uv[0].T.astype(BF16)
    gq = g_q[0][None]
    gkv = g_kv[0][None]
    wo = w_o[0].astype(BF16)
    ne = w_router.shape[-1]
    wr = jnp.zeros((d, LANES), F32).at[:, :ne].set(w_router[0])
    wrh = wr.astype(BF16)
    wrc = jnp.concatenate([wrh, (wr - wrh.astype(F32)).astype(BF16)], axis=1)
    br = jnp.full((1, LANES), NEG_BIG, F32).at[0, :ne].set(b_router[0])
    wegu = w_exp_gu[0].astype(BF16)
    wed = w_exp_down[0].astype(BF16)
    fg = final_g[None]

    tc_p, ts_p = _rope_tables(jnp.arange(seq))
    pm = prompt_mod(1, 0)
    ckv_p, kpe_p, q_p, k_p, vt_p = _mla_prompt(
        xp, pm, pspec(seq // tm_mla), g1, (wdc, gq, gkv, wqa, wqb, wuk, wuvt),
        tc_p, ts_p, tm=tm_mla, tiles_per_seq=seq // tm_mla)
    o_p = _attn_prompt(q_p.reshape(nb, seq, -1), k_p.reshape(nb, seq, -1), vt_p, tq=tq)
    pm6 = mods[1, :nb][:, :, None, :]
    tiles_moe = seq // tmoe
    xp, hp, route_p, cnt_p = _oproj_route(
        xp, pm6, pl.BlockSpec((None, 6, 1, d), lambda i: (i // tiles_moe, 0, 0, 0)),
        o_p.reshape(n_p, -1), wo, g2, wrc, br, tm=tmoe)
    yp = _moe_routed(
        hp, xp, pm6, pl.BlockSpec((None, 6, 1, d), lambda i, j, *_: (i // tiles_moe, 0, 0, 0)),
        route_p, cnt_p, wegu, wed, fg, tm=tmoe)

    tc_s, ts_s = _rope_tables(past + jnp.arange(t_s))
    tc_s = jnp.tile(tc_s, (ns_b, 1))
    ts_s = jnp.tile(ts_s, (ns_b, 1))
    wqn = wq_nope.reshape(Q_LORA, -1).astype(BF16)
    wqp = wq_pe.reshape(Q_LORA, -1).astype(BF16)
    wqs = wq_sw.reshape(Q_LORA, -1).astype(BF16)
    wukt = w_uk[0].reshape(KV_LORA, MLA_HEADS, QK_NOPE).transpose(1, 2, 0).astype(BF16)
    wuv_h = w_uv[0].reshape(KV_LORA, MLA_HEADS, V_HD).transpose(1, 0, 2).astype(BF16)
    sm = sample_mod(1, 0)
    ckv_s, kpe_s, qlat, qpe = _mla_sample(
        xs, sm, sspec(n_s), g1, (wdc, gq, gkv, wqn, wqp, wqs, wukt), tc_s, ts_s, tm=n_s)
    o_lat = _attn_sample(
        qlat.reshape(ns_b, t_s * MLA_HEADS, KV_LORA), qpe.reshape(ns_b, t_s * MLA_HEADS, QK_ROPE),
        cache_ckv[0], jnp.swapaxes(cache_kpe[0], 1, 2), ckv_s.reshape(ns_b, t_s, KV_LORA),
        kpe_s.reshape(ns_b, t_s, QK_ROPE))
    sm6 = jnp.broadcast_to(mods[1, nb:][:, None], (ns_b, t_s, 6, d)).reshape(n_s, 6, d).transpose(1, 0, 2)
    xs, hs, route_s, cnt_s = _oproj_route(
        xs, sm6, pl.BlockSpec((6, n_s, d), lambda i: (0, i, 0)),
        o_lat.reshape(n_s, MLA_HEADS * KV_LORA), wo, g2, wrc, br, wuv_h, tm=n_s)
    ys = _moe_routed(
        hs, xs, sm6, pl.BlockSpec((6, n_s, d), lambda i, j, *_: (0, i, 0)),
        route_s, cnt_s, wegu, wed, fg, tm=n_s)

    return (yp.reshape(nb, seq, d), ys.reshape(ns_b, t_s, d),
            conv_p[None], conv_s[None], gv_s.reshape(ns_b, t_s, c_ch)[None],
            ckv_p.reshape(nb, seq, KV_LORA)[None], kpe_p.reshape(nb, seq, QK_ROPE)[None],
            ckv_s.reshape(ns_b, t_s, KV_LORA)[None], kpe_s.reshape(ns_b, t_s, QK_ROPE)[None])
```

```python
import functools
import math

import jax
import jax.numpy as jnp
from jax import lax
from jax.experimental import pallas as pl
from jax.experimental.pallas import tpu as pltpu

F32 = jnp.float32
BF16 = jnp.bfloat16

EPS = 1e-6
CHUNK = 64
CONV_W = 31
HIST = CONV_W - 1
HIST_PAD = 32
GMLP_CHUNK = 128
GMLP_HEADS = 8
MLA_HEADS = 8
Q_LORA = 256
KV_LORA = 256
QK_NOPE = 128
QK_ROPE = 64
V_HD = 128
V_AUG = V_HD + 16
QK_CAT = 256
ROPE_BASE = 10000.0
N_EXPERTS = 8
MOE_ROWS = 256
MOE_PAD = 64
LANES = 128
SUBLANES = 8
CONV_UNROLL = True
NEG_BIG = -1e30

VMEM_LIMIT = 56 * 1024 * 1024


def _cparams(*sem):
    return pltpu.CompilerParams(dimension_semantics=sem, vmem_limit_bytes=VMEM_LIMIT)


def _const_spec(shape):
    nd = len(shape)
    return pl.BlockSpec(shape, lambda *_: (0,) * nd, pipeline_mode=pl.Buffered(1))


def _modulate(x, g, shift, scale):
    y = x * lax.rsqrt(jnp.mean(x * x, axis=-1, keepdims=True) + EPS)
    return (y * g) * (1.0 + scale) + shift


def _layer_norm(x, g, b):
    mu = jnp.mean(x, axis=-1, keepdims=True)
    xc = x - mu
    var = jnp.mean(xc * xc, axis=-1, keepdims=True)
    return xc * lax.rsqrt(var + EPS) * g + b


def _dot(a, b):
    return jnp.dot(a, b, preferred_element_type=F32)


_NT = (((1,), (1,)), ((), ()))


def _adaln_kernel(c_ref, w_ref, b_ref, o_ref):
    c = c_ref[...]
    a = jax.nn.silu(c).astype(BF16)
    o_ref[...] = _dot(a, w_ref[...].astype(BF16)) + b_ref[...]


def _adaln(c_all, w_ada, b_ada, tn=1536):
    depth, d, n = w_ada.shape
    rows = c_all.shape[0]
    return pl.pallas_call(
        _adaln_kernel,
        grid=(depth, n // tn),
        in_specs=[
            pl.BlockSpec((rows, d), lambda l, j: (0, 0)),
            pl.BlockSpec((None, d, tn), lambda l, j: (l, 0, j)),
            pl.BlockSpec((None, 1, tn), lambda l, j: (l, 0, j)),
        ],
        out_specs=pl.BlockSpec((None, rows, tn), lambda l, j: (l, 0, j)),
        out_shape=jax.ShapeDtypeStruct((depth, rows, n), F32),
        compiler_params=_cparams("parallel", "parallel"),
        name="adaln",
    )(c_all, w_ada, b_ada.reshape(depth, 1, n))


def _mixer_kernel(*refs, ns, t, carry_hist, conv_rows):
    if carry_hist:
        (x_ref, mod_ref, g_ref, win_ref, bin_ref, cw_ref, cb_ref, lag_ref, lab_ref,
         lvg_ref, lvb_ref, wsp_ref, bsp_ref, woa_ref, wob_ref, bout_ref,
         xo_ref, conv_ref, aext_ref, ya_ref, yb_ref, tapwin_ref) = refs
        hist_ref = v_ref = None
    else:
        (x_ref, mod_ref, g_ref, win_ref, bin_ref, cw_ref, cb_ref, lag_ref, lab_ref,
         lvg_ref, lvb_ref, wsp_ref, bsp_ref, woa_ref, wob_ref, bout_ref, hist_ref,
         xo_ref, conv_ref, v_ref, aext_ref, ya_ref, yb_ref, tapwin_ref) = refs
    c = cw_ref.shape[1]
    rows = ns * t

    x = x_ref[...]
    h = _modulate(x, g_ref[...], mod_ref[0], mod_ref[1]).astype(BF16)
    p = _dot(h, win_ref[...]) + bin_ref[...]

    a = p[:, :c] * jax.nn.sigmoid(p[:, c:2 * c])
    if carry_hist:
        @pl.when(pl.program_id(1) == 0)
        def _():
            aext_ref[:, :HIST_PAD, :] = jnp.zeros((ns, HIST_PAD, c), F32)
    else:
        aext_ref[:, HIST_PAD - HIST:HIST_PAD, :] = hist_ref[...]
    aext_ref[:, HIST_PAD:, :] = a.reshape(ns, t, c)

    off = HIST_PAD - HIST
    n_cc = t // conv_rows
    lag = lag_ref[...]
    lab = lab_ref[...]
    cb = cb_ref[...]

    def finish(acc):
        y = _layer_norm(acc + cb, lag, lab)
        return jax.nn.silu(y).astype(BF16)

    def conv_direct(s):
        acc = jnp.zeros((conv_rows, c), F32)
        for k in range(CONV_W):
            acc = acc + aext_ref[s, off + k:off + k + conv_rows, :] * cw_ref[k:k + 1, :]
        return finish(acc)

    def conv_phased(s, r0):
        acc = jnp.zeros((conv_rows, c), F32)
        for k in range(CONV_W):
            r = (off + k) % SUBLANES
            q8 = pl.multiple_of(r0 + (off + k - r), SUBLANES)
            if r == 0:
                tap = aext_ref[s, pl.ds(q8, conv_rows), :]
            else:
                tap = tapwin_ref[r - 1, pl.ds(q8, conv_rows), :]
            acc = acc + tap * cw_ref[k:k + 1, :]
        return finish(acc)

    for s in range(ns):
        if n_cc == 1:
            ya_ref[s * t:(s + 1) * t, :] = conv_direct(s)
        else:
            span = t + HIST_PAD - SUBLANES
            for r in range(1, SUBLANES):
                tapwin_ref[r - 1] = aext_ref[s, r:r + span, :]

            def body(i, carry, s=s):
                r0 = pl.multiple_of(i * conv_rows, conv_rows)
                ya_ref[pl.ds(s * t + r0, conv_rows), :] = conv_phased(s, r0)
                return carry
            lax.fori_loop(0, n_cc, body, 0, unroll=CONV_UNROLL)

    conv_ref[...] = aext_ref[:, t + HIST_PAD - HIST:t + HIST_PAD, :]
    if carry_hist:
        aext_ref[:, :HIST_PAD, :] = aext_ref[:, t:t + HIST_PAD, :]

    z = jax.nn.gelu(p[:, 2 * c:])
    u = z[:, :c]
    v = _layer_norm(z[:, c:], lvg_ref[...], lvb_ref[...])
    if v_ref is not None:
        v_ref[...] = v
    ri = lax.broadcasted_iota(jnp.int32, (GMLP_CHUNK, GMLP_CHUNK), 0)
    ci = lax.broadcasted_iota(jnp.int32, (GMLP_CHUNK, GMLP_CHUNK), 1)
    tril = ri >= ci
    low_half = ci < (LANES // 2)
    w_heads = [jnp.where(tril, wsp_ref[hh], 0.0).astype(BF16) for hh in range(GMLP_HEADS)]
    for blk in range(rows // GMLP_CHUNK):
        r = slice(blk * GMLP_CHUNK, (blk + 1) * GMLP_CHUNK)
        for j in range(c // LANES):
            l = slice(j * LANES, (j + 1) * LANES)
            vp = v[r, l]
            lo = jnp.where(low_half, vp, 0.0).astype(BF16)
            hi = jnp.where(low_half, 0.0, vp).astype(BF16)
            sp = _dot(w_heads[2 * j], lo) + _dot(w_heads[2 * j + 1], hi)
            yb_ref[r, l] = (u[r, l] * (sp + bsp_ref[:, l])).astype(BF16)

    out = _dot(ya_ref[...], woa_ref[...]) + _dot(yb_ref[...], wob_ref[...]) + bout_ref[...]
    xo_ref[...] = x + mod_ref[2] * out


def _mixer(x, mod, mod_spec, g1, wts, *, ns, t, n_outer, n_inner, hist=None):
    (w_in, b_in, cw, cb, lag, lab, lvg, lvb, wsp, bsp, woa, wob, b_out) = wts
    d = x.shape[1]
    c = cw.shape[1]
    rows = ns * t
    carry_hist = hist is None
    tile = lambda o, i: (o * n_inner + i, 0)
    in_specs = [
        pl.BlockSpec((rows, d), tile),
        mod_spec,
        _const_spec((1, d)),
        _const_spec(w_in.shape), _const_spec(b_in.shape),
        _const_spec(cw.shape), _const_spec(cb.shape),
        _const_spec(lag.shape), _const_spec(lab.shape),
        _const_spec(lvg.shape), _const_spec(lvb.shape),
        _const_spec(wsp.shape), _const_spec(bsp.shape),
        _const_spec(woa.shape), _const_spec(wob.shape), _const_spec(b_out.shape),
    ]
    args = [x, mod, g1, w_in, b_in, cw, cb, lag, lab, lvg, lvb, wsp, bsp, woa, wob, b_out]
    n_tiles = n_outer * n_inner
    out_shape = [jax.ShapeDtypeStruct(x.shape, F32)]
    out_specs = [pl.BlockSpec((rows, d), tile)]
    if carry_hist:
        out_shape.append(jax.ShapeDtypeStruct((n_outer * ns, HIST, c), F32))
        out_specs.append(pl.BlockSpec((ns, HIST, c), lambda o, i: (o, 0, 0)))
    else:
        in_specs.append(pl.BlockSpec((ns, HIST, c), lambda o, i: (o * n_inner + i, 0, 0)))
        args.append(hist)
        out_shape.append(jax.ShapeDtypeStruct((n_tiles * ns, HIST, c), F32))
        out_specs.append(pl.BlockSpec((ns, HIST, c), lambda o, i: (o * n_inner + i, 0, 0)))
        out_shape.append(jax.ShapeDtypeStruct((x.shape[0], c), F32))
        out_specs.append(pl.BlockSpec((rows, c), tile))
    conv_rows = min(t, 64)
    return pl.pallas_call(
        functools.partial(_mixer_kernel, ns=ns, t=t, carry_hist=carry_hist, conv_rows=conv_rows),
        grid=(n_outer, n_inner),
        in_specs=in_specs,
        out_specs=out_specs,
        out_shape=out_shape,
        scratch_shapes=[
            pltpu.VMEM((ns, t + HIST_PAD, c), F32),
            pltpu.VMEM((rows, c), BF16),
            pltpu.VMEM((rows, c), BF16),
            pltpu.VMEM((SUBLANES - 1, t + HIST_PAD - SUBLANES, c), F32),
        ],
        compiler_params=_cparams("arbitrary", "arbitrary"),
        name="mixer_prompt" if carry_hist else "mixer_sample",
    )(*args)


def _ffn_kernel(x_ref, mod_ref, g_ref, wgu_ref, wd_ref, o_ref, *, chunk):
    x = x_ref[...]
    h = _modulate(x, g_ref[...], mod_ref[0], mod_ref[1]).astype(BF16)
    ff = wd_ref.shape[0]
    acc = None
    for c0 in range(0, ff, chunk):
        c1 = min(c0 + chunk, ff)
        gate = _dot(h, wgu_ref[:, c0:c1])
        up = _dot(h, wgu_ref[:, ff + c0:ff + c1])
        act = (jax.nn.silu(gate) * up).astype(BF16)
        part = _dot(act, wd_ref[c0:c1, :])
        acc = part if acc is None else acc + part
    o_ref[...] = x + mod_ref[2] * acc


def _ffn(x, mod, mod_spec, g2, wgu, wd, *, tm, chunk):
    n, d = x.shape
    return pl.pallas_call(
        functools.partial(_ffn_kernel, chunk=chunk),
        grid=(n // tm,),
        in_specs=[
            pl.BlockSpec((tm, d), lambda i: (i, 0)),
            mod_spec,
            _const_spec((1, d)),
            _const_spec(wgu.shape), _const_spec(wd.shape),
        ],
        out_specs=pl.BlockSpec((tm, d), lambda i: (i, 0)),
        out_shape=jax.ShapeDtypeStruct((n, d), F32),
        compiler_params=_cparams("parallel"),
        name="ffn",
    )(x, mod, g2, wgu, wd)


def _mla_latents(x_ref, mod_ref, g_ref, wdc_ref, gq_ref, gkv_ref, tc, ts):
    x = x_ref[...]
    h = _modulate(x, g_ref[...], mod_ref[0], mod_ref[1]).astype(BF16)
    p = _dot(h, wdc_ref[...])
    cq = p[:, :Q_LORA]
    cq = cq * lax.rsqrt(jnp.mean(cq * cq, axis=-1, keepdims=True) + EPS) * gq_ref[...]
    ckv = p[:, Q_LORA:Q_LORA + KV_LORA]
    ckv = ckv * lax.rsqrt(jnp.mean(ckv * ckv, axis=-1, keepdims=True) + EPS) * gkv_ref[...]
    b0 = Q_LORA + KV_LORA
    kpe = p[:, b0:b0 + LANES] * tc + p[:, b0 + LANES:b0 + 2 * LANES] * ts
    return cq.astype(BF16), ckv, kpe


def _mla_prompt_kernel(x_ref, mod_ref, g_ref, wdc_ref, gq_ref, gkv_ref, tc_ref, ts_ref,
                       wqa_ref, wqb_ref, wuk_ref, wuvt_ref,
                       ckv_ref, kpe_ref, q_ref, k_ref, vt_ref):
    tc = tc_ref[...]
    ts = ts_ref[...]
    cq, ckv, kpe = _mla_latents(x_ref, mod_ref, g_ref, wdc_ref, gq_ref, gkv_ref, tc, ts)
    ckv_ref[...] = ckv
    kpe_ref[...] = kpe[:, :QK_ROPE]
    qa = _dot(cq, wqa_ref[...])
    qb = _dot(cq, wqb_ref[...])
    ckv_b = ckv.astype(BF16)
    kn = _dot(ckv_b, wuk_ref[...])
    vt = lax.dot_general(wuvt_ref[...], ckv_b, _NT, preferred_element_type=F32).astype(BF16)
    tm = vt.shape[1]
    pad_rows = V_AUG - V_HD
    ones_row = jnp.where(lax.broadcasted_iota(jnp.int32, (pad_rows, tm), 0) == 0, 1.0, 0.0).astype(BF16)
    for hh in range(MLA_HEADS):
        vt_ref[hh * V_AUG:hh * V_AUG + V_HD, :] = vt[hh * V_HD:(hh + 1) * V_HD, :]
        vt_ref[hh * V_AUG + V_HD:(hh + 1) * V_AUG, :] = ones_row
    kpe_b = kpe.astype(BF16)
    for hh in range(MLA_HEADS):
        base = hh * QK_CAT
        q_ref[:, base:base + QK_NOPE] = qa[:, base:base + QK_NOPE].astype(BF16)
        q_ref[:, base + QK_NOPE:base + QK_CAT] = (
            qa[:, base + QK_NOPE:base + QK_CAT] * tc
            + qb[:, hh * LANES:(hh + 1) * LANES] * ts).astype(BF16)
        k_ref[:, base:base + QK_NOPE] = kn[:, hh * QK_NOPE:(hh + 1) * QK_NOPE].astype(BF16)
        k_ref[:, base + QK_NOPE:base + QK_CAT] = kpe_b


def _mla_prompt(x, mod, mod_spec, g1, wts, tc, ts, *, tm, tiles_per_seq):
    wdc, gq, gkv, wqa, wqb, wuk, wuvt = wts
    n, d = x.shape
    hq = MLA_HEADS * QK_CAT
    hv = MLA_HEADS * V_AUG
    nb = n // (tm * tiles_per_seq)
    row = lambda i: (i, 0)
    pos = lambda i: (i % tiles_per_seq, 0)
    return pl.pallas_call(
        _mla_prompt_kernel,
        grid=(n // tm,),
        in_specs=[
            pl.BlockSpec((tm, d), row), mod_spec, _const_spec((1, d)),
            _const_spec(wdc.shape), _const_spec(gq.shape), _const_spec(gkv.shape),
            pl.BlockSpec((tm, LANES), pos), pl.BlockSpec((tm, LANES), pos),
            _const_spec(wqa.shape), _const_spec(wqb.shape),
            _const_spec(wuk.shape), _const_spec(wuvt.shape),
        ],
        out_specs=[
            pl.BlockSpec((tm, KV_LORA), row), pl.BlockSpec((tm, QK_ROPE), row),
            pl.BlockSpec((tm, hq), row), pl.BlockSpec((tm, hq), row),
            pl.BlockSpec((None, hv, tm), lambda i: (i // tiles_per_seq, 0, i % tiles_per_seq)),
        ],
        out_shape=[
            jax.ShapeDtypeStruct((n, KV_LORA), F32), jax.ShapeDtypeStruct((n, QK_ROPE), F32),
            jax.ShapeDtypeStruct((n, hq), BF16), jax.ShapeDtypeStruct((n, hq), BF16),
            jax.ShapeDtypeStruct((nb, hv, tm * tiles_per_seq), BF16),
        ],
        compiler_params=_cparams("parallel"),
        name="mla_proj_prompt",
    )(x, mod, g1, wdc, gq, gkv, tc, ts, wqa, wqb, wuk, wuvt)


def _mla_sample_kernel(x_ref, mod_ref, g_ref, wdc_ref, gq_ref, gkv_ref, tc_ref, ts_ref,
                       wqn_ref, wqp_ref, wqs_ref, wukt_ref,
                       ckv_ref, kpe_ref, qlat_ref, qpe_ref):
    tc = tc_ref[...]
    ts = ts_ref[...]
    cq, ckv, kpe = _mla_latents(x_ref, mod_ref, g_ref, wdc_ref, gq_ref, gkv_ref, tc, ts)
    ckv_ref[...] = ckv
    kpe_ref[...] = kpe[:, :QK_ROPE]
    qn = _dot(cq, wqn_ref[...]).astype(BF16)
    qp = _dot(cq, wqp_ref[...])
    qs = _dot(cq, wqs_ref[...])
    for j in range(MLA_HEADS * QK_ROPE // LANES):
        l = slice(j * LANES, (j + 1) * LANES)
        qpe_ref[:, l] = (qp[:, l] * tc + qs[:, l] * ts).astype(BF16)
    for hh in range(MLA_HEADS):
        qlat_ref[:, hh * KV_LORA:(hh + 1) * KV_LORA] = _dot(
            qn[:, hh * QK_NOPE:(hh + 1) * QK_NOPE], wukt_ref[hh]).astype(BF16)


def _mla_sample(x, mod, mod_spec, g1, wts, tc, ts, *, tm):
    wdc, gq, gkv, wqn, wqp, wqs, wukt = wts
    n, d = x.shape
    row = lambda i: (i, 0)
    return pl.pallas_call(
        _mla_sample_kernel,
        grid=(n // tm,),
        in_specs=[
            pl.BlockSpec((tm, d), row), mod_spec, _const_spec((1, d)),
            _const_spec(wdc.shape), _const_spec(gq.shape), _const_spec(gkv.shape),
            pl.BlockSpec((tm, LANES), row), pl.BlockSpec((tm, LANES), row),
            _const_spec(wqn.shape), _const_spec(wqp.shape), _const_spec(wqs.shape),
            _const_spec(wukt.shape),
        ],
        out_specs=[
            pl.BlockSpec((tm, KV_LORA), row), pl.BlockSpec((tm, QK_ROPE), row),
            pl.BlockSpec((tm, MLA_HEADS * KV_LORA), row),
            pl.BlockSpec((tm, MLA_HEADS * QK_ROPE), row),
        ],
        out_shape=[
            jax.ShapeDtypeStruct((n, KV_LORA), F32), jax.ShapeDtypeStruct((n, QK_ROPE), F32),
            jax.ShapeDtypeStruct((n, MLA_HEADS * KV_LORA), BF16),
            jax.ShapeDtypeStruct((n, MLA_HEADS * QK_ROPE), BF16),
        ],
        compiler_params=_cparams("parallel"),
        name="mla_proj_sample",
    )(x, mod, g1, wdc, gq, gkv, tc, ts, wqn, wqp, wqs, wukt)


def _attn_prompt_kernel(q_ref, k_ref, vt_ref, o_ref, s_scr, p_scr, acc_scr, *, tq, tk):
    qi = pl.program_id(2)
    n_full = 2 * qi

    def qk_to(slot, t):
        k = k_ref[pl.ds(pl.multiple_of(t * tk, tk), tk), :]
        s = lax.dot_general(k, q_ref[...], _NT, preferred_element_type=F32)
        s_scr[slot] = s
        return jnp.max(s, axis=0, keepdims=True)

    def softmax_from(slot, m, bm):
        m_new = jnp.maximum(m, bm)
        alpha = jnp.exp2(m - m_new)
        p_scr[slot] = jnp.exp2(s_scr[slot] - m_new).astype(BF16)
        return m_new, alpha

    def pv_from(slot, t, alpha):
        vt = vt_ref[:, pl.ds(pl.multiple_of(t * tk, tk), tk)]
        acc_scr[...] = alpha * acc_scr[...] + _dot(vt, p_scr[slot])

    p_scr[1] = jnp.zeros((tk, tq), BF16)
    acc_scr[...] = jnp.zeros((V_AUG, tq), F32)
    m0 = jnp.full((1, tq), NEG_BIG, F32)
    bm0 = qk_to(0, 0)

    def body(i, carry):
        m, bm, a_prev = carry
        t = 2 * i
        bm1 = qk_to(1, t + 1)
        m, a0 = softmax_from(0, m, bm)
        pv_from(1, jnp.maximum(t - 1, 0), a_prev)
        bm2 = qk_to(0, t + 2)
        m, a1 = softmax_from(1, m, bm1)
        pv_from(0, t, a0)
        return m, bm2, a1

    m, bm, a_prev = lax.fori_loop(0, qi, body, (m0, bm0, jnp.ones((1, tq), F32)))
    pv_from(1, jnp.maximum(n_full - 1, 0), a_prev)
    kk = lax.broadcasted_iota(jnp.int32, (tk, tk), 0) // CHUNK
    qq = lax.broadcasted_iota(jnp.int32, (tk, tk), 1) // CHUNK
    vis = qq >= kk
    s_a = jnp.where(vis, s_scr[0, :, :tk], NEG_BIG)
    bm_d = jnp.concatenate([jnp.max(s_a, axis=0, keepdims=True), bm[:, tk:]], axis=1)
    m_new = jnp.maximum(m, bm_d)
    alpha = jnp.exp2(m - m_new)
    p_a = jnp.exp2(s_a - m_new[:, :tk])
    p_b = jnp.exp2(s_scr[0, :, tk:] - m_new[:, tk:])
    p_scr[0, :, :tk] = p_a.astype(BF16)
    p_scr[0, :, tk:] = p_b.astype(BF16)
    pv_from(0, n_full, alpha)
    k1 = k_ref[pl.ds(pl.multiple_of((n_full + 1) * tk, tk), tk), :]
    s1 = lax.dot_general(k1, q_ref[tk:, :], _NT, preferred_element_type=F32)
    s1 = jnp.where(vis, s1, NEG_BIG)
    m_b = m_new[:, tk:]
    m_b2 = jnp.maximum(m_b, jnp.max(s1, axis=0, keepdims=True))
    a1 = jnp.exp2(m_b - m_b2)
    p1 = jnp.exp2(s1 - m_b2)
    vt1 = vt_ref[:, pl.ds(pl.multiple_of((n_full + 1) * tk, tk), tk)]
    acc_b = a1 * acc_scr[:, tk:] + _dot(vt1, p1.astype(BF16))
    o_ref[:tk, :] = (acc_scr[:V_HD, :tk] / acc_scr[V_HD:V_HD + 1, :tk]).T.astype(o_ref.dtype)
    o_ref[tk:, :] = (acc_b[:V_HD] / acc_b[V_HD:V_HD + 1]).T.astype(o_ref.dtype)


def _attn_prompt(q, k, vt, *, tq):
    b, s, _ = q.shape
    tk = tq // 2
    return pl.pallas_call(
        functools.partial(_attn_prompt_kernel, tq=tq, tk=tk),
        grid=(b, MLA_HEADS, s // tq),
        in_specs=[
            pl.BlockSpec((None, tq, QK_CAT), lambda bi, hi, qi: (bi, qi, hi)),
            pl.BlockSpec((None, s, QK_CAT), lambda bi, hi, qi: (bi, 0, hi)),
            pl.BlockSpec((None, V_AUG, s), lambda bi, hi, qi: (bi, hi, 0)),
        ],
        out_specs=pl.BlockSpec((None, tq, V_HD), lambda bi, hi, qi: (bi, qi, hi)),
        out_shape=jax.ShapeDtypeStruct((b, s, MLA_HEADS * V_HD), BF16),
        scratch_shapes=[pltpu.VMEM((2, tk, tq), F32), pltpu.VMEM((2, tk, tq), BF16),
                        pltpu.VMEM((V_AUG, tq), F32)],
        compiler_params=_cparams("parallel", "parallel", "arbitrary"),
        name="attn_prompt",
    )(q, k, vt)


def _attn_sample_kernel(ql_ref, qp_ref, cc_ref, cpt_ref, nc_ref, np_ref, o_ref):
    for g in range(ql_ref.shape[0]):
        ql = ql_ref[g]
        qp = qp_ref[g]
        cc = cc_ref[g].astype(BF16)
        cpt = cpt_ref[g].astype(BF16)
        nc = nc_ref[g].astype(BF16)
        npe = np_ref[g].astype(BF16)
        s_c = lax.dot_general(ql, cc, _NT, preferred_element_type=F32) + _dot(qp, cpt)
        s_n = (lax.dot_general(ql, nc, _NT, preferred_element_type=F32)
               + lax.dot_general(qp, npe, _NT, preferred_element_type=F32))
        m = jnp.maximum(jnp.max(s_c, axis=-1, keepdims=True), jnp.max(s_n, axis=-1, keepdims=True))
        p_c = jnp.exp2(s_c - m)
        p_n = jnp.exp2(s_n - m)
        l = jnp.sum(p_c, axis=-1, keepdims=True) + jnp.sum(p_n, axis=-1, keepdims=True)
        o = _dot(p_c.astype(BF16), cc) + _dot(p_n.astype(BF16), nc)
        o_ref[g] = (o / l).astype(o_ref.dtype)


def _attn_sample(qlat, qpe, cache_ckv, cache_kpe_t, ckv_new, kpe_new, *, streams=2):
    nb, past, _ = cache_ckv.shape
    r = qlat.shape[1]
    t = ckv_new.shape[1]
    blk = lambda shape: pl.BlockSpec((streams,) + shape, lambda i: (i, 0, 0))
    return pl.pallas_call(
        _attn_sample_kernel,
        grid=(nb // streams,),
        in_specs=[blk((r, KV_LORA)), blk((r, QK_ROPE)), blk((past, KV_LORA)),
                  blk((QK_ROPE, past)), blk((t, KV_LORA)), blk((t, QK_ROPE))],
        out_specs=blk((r, KV_LORA)),
        out_shape=jax.ShapeDtypeStruct((nb, r, KV_LORA), BF16),
        compiler_params=_cparams("parallel"),
        name="attn_sample",
    )(qlat, qpe, cache_ckv, cache_kpe_t, ckv_new, kpe_new)


def _route_tail(xm, mod_ref, g_ref, wrc_ref, br_ref, xo_ref, h_ref, route_ref, routet_ref, cnt_ref):
    tm = xm.shape[0]
    xo_ref[...] = xm
    hf = _modulate(xm, g_ref[...], mod_ref[3], mod_ref[4])
    hb = hf.astype(BF16)
    h_ref[...] = hb
    h_lo = (hf - hb.astype(F32)).astype(BF16)
    both = _dot(hb, wrc_ref[...])
    logits = (both[:, :LANES] + both[:, LANES:] + _dot(h_lo, wrc_ref[:, :LANES])) + br_ref[...]
    lane = lax.broadcasted_iota(jnp.int32, (tm, LANES), 1)
    m1 = jnp.max(logits, axis=-1, keepdims=True)
    i1 = jnp.min(jnp.where(logits == m1, lane, LANES), axis=-1, keepdims=True)
    rest = jnp.where(lane == i1, -jnp.inf, logits)
    m2 = jnp.max(rest, axis=-1, keepdims=True)
    i2 = jnp.min(jnp.where(rest == m2, lane, LANES), axis=-1, keepdims=True)
    e2 = jnp.exp(m2 - m1)
    den = 1.0 + e2
    sel1 = lane == i1
    sel2 = lane == i2
    onehot = jnp.where(sel1, 1.0, jnp.where(sel2, 1.0, 0.0))
    ri = lax.broadcasted_iota(jnp.int32, (tm, tm), 0)
    ci = lax.broadcasted_iota(jnp.int32, (tm, tm), 1)
    before = jnp.where(ri > ci, 1.0, 0.0).astype(BF16)
    rank = _dot(before, onehot.astype(BF16))
    cnt = jnp.sum(onehot, axis=0, keepdims=True)
    nblk = jnp.floor((cnt + (MOE_PAD - 1)) * (1.0 / MOE_PAD))
    r8 = lax.broadcasted_iota(jnp.int32, (LANES, LANES), 0)
    c8 = lax.broadcasted_iota(jnp.int32, (LANES, LANES), 1)
    upper = jnp.where(r8 < c8, 1.0, 0.0).astype(BF16)
    first_blk = _dot(jnp.broadcast_to(nblk, (8, LANES)).astype(BF16), upper)[0:1]
    slot = first_blk * MOE_PAD + rank
    d1 = jnp.sum(jnp.where(sel1, slot, 0.0), axis=-1, keepdims=True)
    d2 = jnp.sum(jnp.where(sel2, slot, 0.0), axis=-1, keepdims=True)
    route = jnp.where(lane == 0, d1, jnp.where(lane == 1, d2, jnp.where(
        lane == 2, 1.0 / den, jnp.where(lane == 3, e2 / den, 0.0))))
    route_ref[...] = route
    routet_ref[...] = route.T[:8]
    cnt_ref[...] = jnp.broadcast_to(cnt, (8, LANES))


def _oproj_route_kernel(x_ref, mod_ref, o_ref, wo_ref, g_ref, wrc_ref, br_ref,
                        xo_ref, h_ref, route_ref, routet_ref, cnt_ref):
    xm = x_ref[...] + mod_ref[2] * _dot(o_ref[...], wo_ref[...])
    _route_tail(xm, mod_ref, g_ref, wrc_ref, br_ref, xo_ref, h_ref, route_ref, routet_ref, cnt_ref)


def _oproj_latent_route_kernel(x_ref, mod_ref, o_ref, wuv_ref, wo_ref, g_ref, wrc_ref,
                               br_ref, xo_ref, h_ref, route_ref, routet_ref, cnt_ref):
    acc = None
    for hh in range(MLA_HEADS):
        oh = _dot(o_ref[:, hh * KV_LORA:(hh + 1) * KV_LORA], wuv_ref[hh]).astype(BF16)
        part = _dot(oh, wo_ref[hh * V_HD:(hh + 1) * V_HD, :])
        acc = part if acc is None else acc + part
    xm = x_ref[...] + mod_ref[2] * acc
    _route_tail(xm, mod_ref, g_ref, wrc_ref, br_ref, xo_ref, h_ref, route_ref, routet_ref, cnt_ref)


def _oproj_route(x, mod, mod_spec, o, wo, g2, wrc, br, wuv=None, *, tm):
    n, d = x.shape
    row = lambda i: (i, 0)
    in_specs = [pl.BlockSpec((tm, d), row), mod_spec, pl.BlockSpec((tm, o.shape[1]), row)]
    args = [x, mod, o]
    if wuv is not None:
        in_specs.append(_const_spec(wuv.shape))
        args.append(wuv)
    in_specs += [_const_spec(wo.shape), _const_spec((1, d)), _const_spec(wrc.shape),
                 _const_spec(br.shape)]
    args += [wo, g2, wrc, br]
    return pl.pallas_call(
        _oproj_route_kernel if wuv is None else _oproj_latent_route_kernel,
        grid=(n // tm,),
        in_specs=in_specs,
        out_specs=[pl.BlockSpec((tm, d), row), pl.BlockSpec((tm, d), row),
                   pl.BlockSpec((tm, LANES), row), pl.BlockSpec((None, 8, tm), lambda i: (i, 0, 0)),
                   pl.BlockSpec((None, 8, LANES), lambda i: (i, 0, 0))],
        out_shape=[jax.ShapeDtypeStruct((n, d), F32), jax.ShapeDtypeStruct((n, d), BF16),
                   jax.ShapeDtypeStruct((n, LANES), F32),
                   jax.ShapeDtypeStruct((n // tm, 8, tm), F32),
                   jax.ShapeDtypeStruct((n // tm, 8, LANES), F32)],
        compiler_params=_cparams("parallel"),
        name="oproj_route" if wuv is None else "oproj_latent_route",
    )(*args)


def _moe_routed_kernel(row0_ref, npad_ref, h_ref, x_ref, mod_ref, rc_ref, rr_ref,
                       wgu_ref, wd_ref, fg_ref, y_ref, acc_ref):
    i = pl.program_id(0)
    e = pl.program_id(1)
    n_exp = pl.num_programs(1)
    tm = h_ref.shape[0]
    start = row0_ref[i * n_exp + e]
    n_pad = npad_ref[i * n_exp + e]
    per = MOE_ROWS // MOE_PAD
    has_head = n_pad % per

    @pl.when(e == 0)
    def _():
        acc_ref[...] = jnp.zeros(acc_ref.shape, F32)

    def block(rows, first_slot):
        base = first_slot.astype(F32)
        rr = rr_ref[...]
        srow = lax.broadcasted_iota(jnp.int32, (rows, tm), 0).astype(F32) + base
        hit1 = rr[0:1, :] == srow
        hit2 = rr[1:2, :] == srow
        sel = jnp.where(hit1, 1.0, jnp.where(hit2, 1.0, 0.0)).astype(BF16)
        gate = jnp.sum(jnp.where(hit1, rr[2:3, :], jnp.where(hit2, rr[3:4, :], 0.0)),
                       axis=-1, keepdims=True)
        hb = _dot(sel, h_ref[...]).astype(BF16)
        gu = _dot(hb, wgu_ref[...])
        ff = gu.shape[1] // 2
        act = (jax.nn.silu(gu[:, :ff]) * gu[:, ff:]).astype(BF16)
        yb = (_dot(act, wd_ref[...]) * gate).astype(BF16)
        rc = rc_ref[...]
        scol = lax.broadcasted_iota(jnp.int32, (tm, rows), 1).astype(F32) + base
        sel_t = jnp.where(rc[:, 0:1] == scol, 1.0,
                          jnp.where(rc[:, 1:2] == scol, 1.0, 0.0)).astype(BF16)
        acc_ref[...] += _dot(sel_t, yb)

    for units in range(1, per):
        @pl.when(has_head == units)
        def _(units=units):
            block(units * MOE_PAD, start)

    def body(b, carry):
        block(MOE_ROWS, start + has_head * MOE_PAD + b * MOE_ROWS)
        return carry

    lax.fori_loop(0, n_pad // per, body, 0)

    @pl.when(e == n_exp - 1)
    def _():
        xn = x_ref[...] + mod_ref[5] * acc_ref[...]
        y_ref[...] = xn * lax.rsqrt(jnp.mean(xn * xn, axis=-1, keepdims=True) + EPS) * fg_ref[...]


def _moe_routed(h, x, mod, mod_spec2, route, route_row, cnt, wgu, wd, fg, *, tm):
    n, d = x.shape
    ne, ff, _ = wd.shape
    n_tiles = n // tm
    n_pad = jnp.ceil(cnt[:, 0, :ne] / MOE_PAD).astype(jnp.int32)
    row0 = (jnp.cumsum(n_pad, axis=1) - n_pad) * MOE_PAD
    wspec = lambda shape: pl.BlockSpec((None,) + shape, lambda i, e, r0, npd: (e, 0, 0))
    once = lambda shape: pl.BlockSpec(shape, lambda i, e, r0, npd: (i, 0), pipeline_mode=pl.Buffered(1))
    return pl.pallas_call(
        _moe_routed_kernel,
        grid_spec=pltpu.PrefetchScalarGridSpec(
            num_scalar_prefetch=2,
            grid=(n_tiles, ne),
            in_specs=[
                pl.BlockSpec((tm, d), lambda i, e, r0, npd: (i, 0)), once((tm, d)), mod_spec2,
                pl.BlockSpec((tm, LANES), lambda i, e, r0, npd: (i, 0)),
                pl.BlockSpec((None, 8, tm), lambda i, e, r0, npd: (i, 0, 0)),
                wspec((d, 2 * ff)), wspec((ff, d)),
                pl.BlockSpec((1, d), lambda i, e, r0, npd: (0, 0)),
            ],
            out_specs=pl.BlockSpec((tm, d), lambda i, e, r0, npd: (i, 0)),
            scratch_shapes=[pltpu.VMEM((tm, d), F32)],
        ),
        out_shape=jax.ShapeDtypeStruct((n, d), F32),
        compiler_params=_cparams("parallel", "arbitrary"),
        name="moe_routed",
    )(row0.reshape(-1), n_pad.reshape(-1), h, x, mod, route, route_row, wgu, wd, fg)


def _rope_tables(pos):
    half = QK_ROPE // 2
    inv = jnp.exp(-math.log(ROPE_BASE) * jnp.arange(half, dtype=F32) / half)
    ang = pos.astype(F32)[:, None] * inv[None, :]
    cos, sin = jnp.cos(ang), jnp.sin(ang)
    reps = LANES // QK_ROPE
    tc = jnp.tile(jnp.concatenate([cos, cos], axis=-1), (1, reps))
    ts = jnp.tile(jnp.concatenate([-sin, sin], axis=-1), (1, reps))
    return tc, ts


def _swap_halves(w):
    half = w.shape[-1] // 2
    return jnp.concatenate([w[..., half:], w[..., :half]], axis=-1)


def kernel(x_prompt, x_sample, c_prompt, c_sample, state_conv, cache_ckv, cache_kpe, norm1_g, norm2_g, w_ada, b_ada, w_in_ab, b_in_ab, conv_w, conv_b, ln_conv_g, ln_conv_b, ln_v_g, ln_v_b, w_spatial, b_spatial, w_out_ab, b_out_ab, w_ffn_gu, w_ffn_down, w_dc, g_q, g_kv, w_uq, w_uk, w_uv, w_o, w_router, b_router, w_exp_gu, w_exp_down, final_g):
    nb, seq, d = x_prompt.shape
    ns_b, t_s, _ = x_sample.shape
    past = cache_ckv.shape[2]
    c_ch = conv_w.shape[-1]
    n_p = nb * seq
    n_s = ns_b * t_s
    tm = 1024
    tq = 2048
    tmoe = 1024
    tm_mla = 1024
    tiles_per_seq = seq // tm
    s_group = GMLP_CHUNK // t_s

    xp = x_prompt.reshape(n_p, d)
    xs = x_sample.reshape(n_s, d)

    mods = _adaln(jnp.concatenate([c_prompt, c_sample], axis=0), w_ada, b_ada)
    depth = w_ada.shape[0]
    mods = mods.reshape(depth, nb + ns_b, 6, d)

    def prompt_mod(l, k0):
        return mods[l, :nb, k0:k0 + 3][:, :, None, :]

    def sample_mod(l, k0):
        m = mods[l, nb:, k0:k0 + 3]
        m = jnp.broadcast_to(m[:, None], (ns_b, t_s, 3, d)).reshape(n_s, 3, d)
        return m.transpose(1, 0, 2)

    def pspec(tiles_per_b):
        return pl.BlockSpec((None, 3, 1, d), lambda i: (i // tiles_per_b, 0, 0, 0))

    def sspec(rows):
        return pl.BlockSpec((3, rows, d), lambda i: (0, i, 0))

    g1 = norm1_g[0][None]
    g2 = norm2_g[0][None]
    w_in = w_in_ab[0].astype(BF16)
    woa = w_out_ab[0][:c_ch].astype(BF16)
    wob = w_out_ab[0][c_ch:].astype(BF16)
    hd = c_ch // GMLP_HEADS
    bsp_p = jnp.repeat(b_spatial[0].T, hd, axis=1)
    wsp_p = w_spatial[0]
    eye = jnp.eye(s_group, dtype=F32)
    wsp_s = jnp.einsum('ab,hts->hatbs', eye, w_spatial[0][:, :t_s, :t_s]).reshape(
        GMLP_HEADS, GMLP_CHUNK, GMLP_CHUNK)
    bsp_s = jnp.tile(jnp.repeat(b_spatial[0][:, :t_s].T, hd, axis=1), (s_group, 1))
    common = (w_in, b_in_ab[0][None], conv_w[0], conv_b[0][None], ln_conv_g[0][None],
              ln_conv_b[0][None], ln_v_g[0][None], ln_v_b[0][None])
    tail = (woa, wob, b_out_ab[0][None])

    pm = prompt_mod(0, 0)
    xp, conv_p = _mixer(
        xp, pm, pl.BlockSpec((None, 3, 1, d), lambda o, i: (o, 0, 0, 0)), g1,
        common + (wsp_p, bsp_p) + tail, ns=1, t=tm, n_outer=nb, n_inner=tiles_per_seq)
    sm = sample_mod(0, 0)
    n_st = n_s // GMLP_CHUNK
    xs, conv_s, gv_s = _mixer(
        xs, sm, pl.BlockSpec((3, GMLP_CHUNK, d), lambda o, i: (0, o * n_st + i, 0)), g1,
        common + (wsp_s, bsp_s) + tail, ns=s_group, t=t_s, n_outer=1, n_inner=n_st,
        hist=state_conv[0])

    ff = w_ffn_gu.shape[-1] // 2
    wgu = w_ffn_gu[0].astype(BF16)
    wd = w_ffn_down[0].astype(BF16)
    tm_ffn = 1024
    xp = _ffn(xp, prompt_mod(0, 3), pspec(seq // tm_ffn), g2, wgu, wd, tm=tm_ffn, chunk=1024)
    xs = _ffn(xs, sample_mod(0, 3), sspec(n_s), g2, wgu, wd, tm=n_s, chunk=ff // 2)

    g1 = norm1_g[1][None]
    g2 = norm2_g[1][None]
    scale = (QK_NOPE + QK_ROPE) ** -0.5 * math.log2(math.e)
    zeros64 = jnp.zeros((d, QK_ROPE), F32)
    w_kpe = w_dc[0][:, Q_LORA + KV_LORA:]
    wdc = jnp.concatenate([w_dc[0][:, :Q_LORA + KV_LORA], w_kpe, zeros64,
                           _swap_halves(w_kpe), zeros64], axis=1).astype(BF16)
    wq3 = w_uq[0].reshape(Q_LORA, MLA_HEADS, QK_NOPE + QK_ROPE) * scale
    wq_nope = wq3[:, :, :QK_NOPE]
    wq_pe = wq3[:, :, QK_NOPE:]
    wq_sw = _swap_halves(wq_pe)
    zpad = jnp.zeros((Q_LORA, MLA_HEADS, QK_ROPE), F32)
    wqa = jnp.concatenate([wq_nope, wq_pe, zpad], axis=-1).reshape(Q_LORA, -1).astype(BF16)
    wqb = jnp.concatenate([wq_sw, zpad], axis=-1).reshape(Q_LORA, -1).astype(BF16)
    wuk = w_uk[0].astype(BF16)
    wuvt = w_uv[0].T.astype(BF16)
    gq = g_q[0][None]
    gkv = g_kv[0][None]
    wo = w_o[0].astype(BF16)
    ne = w_router.shape[-1]
    wr = jnp.zeros((d, LANES), F32).at[:, :ne].set(w_router[0])
    wrh = wr.astype(BF16)
    wrc = jnp.concatenate([wrh, (wr - wrh.astype(F32)).astype(BF16)], axis=1)
    br = jnp.full((1, LANES), NEG_BIG, F32).at[0, :ne].set(b_router[0])
    wegu = w_exp_gu[0].astype(BF16)
    wed = w_exp_down[0].astype(BF16)
    fg = final_g[None]

    tc_p, ts_p = _rope_tables(jnp.arange(seq))
    pm = prompt_mod(1, 0)
    ckv_p, kpe_p, q_p, k_p, vt_p = _mla_prompt(
        xp, pm, pspec(seq // tm_mla), g1, (wdc, gq, gkv, wqa, wqb, wuk, wuvt),
        tc_p, ts_p, tm=tm_mla, tiles_per_seq=seq // tm_mla)
    o_p = _attn_prompt(q_p.reshape(nb, seq, -1), k_p.reshape(nb, seq, -1), vt_p, tq=tq)
    pm6 = mods[1, :nb][:, :, None, :]
    tiles_moe = seq // tmoe
    xp, hp, route_p, routet_p, cnt_p = _oproj_route(
        xp, pm6, pl.BlockSpec((None, 6, 1, d), lambda i: (i // tiles_moe, 0, 0, 0)),
        o_p.reshape(n_p, -1), wo, g2, wrc, br, tm=tmoe)
    yp = _moe_routed(
        hp, xp, pm6, pl.BlockSpec((None, 6, 1, d), lambda i, j, *_: (i // tiles_moe, 0, 0, 0)),
        route_p, routet_p, cnt_p, wegu, wed, fg, tm=tmoe)

    tc_s, ts_s = _rope_tables(past + jnp.arange(t_s))
    tc_s = jnp.tile(tc_s, (ns_b, 1))
    ts_s = jnp.tile(ts_s, (ns_b, 1))
    wqn = wq_nope.reshape(Q_LORA, -1).astype(BF16)
    wqp = wq_pe.reshape(Q_LORA, -1).astype(BF16)
    wqs = wq_sw.reshape(Q_LORA, -1).astype(BF16)
    wukt = w_uk[0].reshape(KV_LORA, MLA_HEADS, QK_NOPE).transpose(1, 2, 0).astype(BF16)
    wuv_h = w_uv[0].reshape(KV_LORA, MLA_HEADS, V_HD).transpose(1, 0, 2).astype(BF16)
    sm = sample_mod(1, 0)
    ckv_s, kpe_s, qlat, qpe = _mla_sample(
        xs, sm, sspec(n_s), g1, (wdc, gq, gkv, wqn, wqp, wqs, wukt), tc_s, ts_s, tm=n_s)
    o_lat = _attn_sample(
        qlat.reshape(ns_b, t_s * MLA_HEADS, KV_LORA), qpe.reshape(ns_b, t_s * MLA_HEADS, QK_ROPE),
        cache_ckv[0], jnp.swapaxes(cache_kpe[0], 1, 2), ckv_s.reshape(ns_b, t_s, KV_LORA),
        kpe_s.reshape(ns_b, t_s, QK_ROPE))
    sm6 = jnp.broadcast_to(mods[1, nb:][:, None], (ns_b, t_s, 6, d)).reshape(n_s, 6, d).transpose(1, 0, 2)
    xs, hs, route_s, routet_s, cnt_s = _oproj_route(
        xs, sm6, pl.BlockSpec((6, n_s, d), lambda i: (0, i, 0)),
        o_lat.reshape(n_s, MLA_HEADS * KV_LORA), wo, g2, wrc, br, wuv_h, tm=n_s)
    ys = _moe_routed(
        hs, xs, sm6, pl.BlockSpec((6, n_s, d), lambda i, j, *_: (0, i, 0)),
        route_s, routet_s, cnt_s, wegu, wed, fg, tm=n_s)

    return (yp.reshape(nb, seq, d), ys.reshape(ns_b, t_s, d),
            conv_p[None], conv_s[None], gv_s.reshape(ns_b, t_s, c_ch)[None],
            ckv_p.reshape(nb, seq, KV_LORA)[None], kpe_p.reshape(nb, seq, QK_ROPE)[None],
            ckv_s.reshape(ns_b, t_s, KV_LORA)[None], kpe_s.reshape(ns_b, t_s, QK_ROPE)[None])
```

```python
import functools
import math

import jax
import jax.numpy as jnp
from jax import lax
from jax.experimental import pallas as pl
from jax.experimental.pallas import tpu as pltpu

F32 = jnp.float32
BF16 = jnp.bfloat16

EPS = 1e-6
CHUNK = 64
CONV_W = 31
HIST = CONV_W - 1
HIST_PAD = 32
GMLP_CHUNK = 128
GMLP_HEADS = 8
MLA_HEADS = 8
Q_LORA = 256
KV_LORA = 256
QK_NOPE = 128
QK_ROPE = 64
V_HD = 128
QK_CAT = 256
ROPE_BASE = 10000.0
N_EXPERTS = 8
LANES = 128
SUBLANES = 8
BF16_ROWS = 16
V_AUG = V_HD + BF16_ROWS
MOE_ROWS = 256
MOE_PAD = 64
CONV_ROWS = 64
CONV_UNROLL = True
NEG_BIG = -1e30

VMEM_LIMIT = 56 * 1024 * 1024

TILE_MIXER = 1024
TILE_FFN = 1024
FFN_CHUNK = 1024
TILE_MLA = 1024
TILE_Q = 2048
TILE_MOE = 1024
ADALN_COLS = 1536
SAMPLE_STREAMS = 2


def _cparams(*sem):
    return pltpu.CompilerParams(dimension_semantics=sem, vmem_limit_bytes=VMEM_LIMIT)


def _const_spec(shape):
    nd = len(shape)
    return pl.BlockSpec(shape, lambda *_: (0,) * nd, pipeline_mode=pl.Buffered(1))


def _modulate(x, g, shift, scale):
    y = x * lax.rsqrt(jnp.mean(x * x, axis=-1, keepdims=True) + EPS)
    return (y * g) * (1.0 + scale) + shift


def _layer_norm(x, g, b):
    mu = jnp.mean(x, axis=-1, keepdims=True)
    xc = x - mu
    var = jnp.mean(xc * xc, axis=-1, keepdims=True)
    return xc * lax.rsqrt(var + EPS) * g + b


def _dot(a, b):
    return jnp.dot(a, b, preferred_element_type=F32)


_NT = (((1,), (1,)), ((), ()))


def _adaln_kernel(c_ref, w_ref, b_ref, o_ref):
    c = c_ref[...]
    a = jax.nn.silu(c).astype(BF16)
    o_ref[...] = _dot(a, w_ref[...].astype(BF16)) + b_ref[...]


def _adaln(c_all, w_ada, b_ada, tn=ADALN_COLS):
    depth, d, n = w_ada.shape
    rows = c_all.shape[0]
    return pl.pallas_call(
        _adaln_kernel,
        grid=(depth, n // tn),
        in_specs=[
            pl.BlockSpec((rows, d), lambda l, j: (0, 0)),
            pl.BlockSpec((None, d, tn), lambda l, j: (l, 0, j)),
            pl.BlockSpec((None, 1, tn), lambda l, j: (l, 0, j)),
        ],
        out_specs=pl.BlockSpec((None, rows, tn), lambda l, j: (l, 0, j)),
        out_shape=jax.ShapeDtypeStruct((depth, rows, n), F32),
        compiler_params=_cparams("parallel", "parallel"),
        name="adaln",
    )(c_all, w_ada, b_ada.reshape(depth, 1, n))


def _mixer_kernel(*refs, ns, t, carry_hist, conv_rows):
    if carry_hist:
        (x_ref, mod_ref, g_ref, win_ref, bin_ref, cw_ref, cb_ref, lag_ref, lab_ref,
         lvg_ref, lvb_ref, wsp_ref, bsp_ref, woa_ref, wob_ref, bout_ref,
         xo_ref, conv_ref, aext_ref, ya_ref, yb_ref, tapwin_ref) = refs
        hist_ref = v_ref = None
    else:
        (x_ref, mod_ref, g_ref, win_ref, bin_ref, cw_ref, cb_ref, lag_ref, lab_ref,
         lvg_ref, lvb_ref, wsp_ref, bsp_ref, woa_ref, wob_ref, bout_ref, hist_ref,
         xo_ref, conv_ref, v_ref, aext_ref, ya_ref, yb_ref, tapwin_ref) = refs
    c = cw_ref.shape[1]
    rows = ns * t

    x = x_ref[...]
    h = _modulate(x, g_ref[...], mod_ref[0], mod_ref[1]).astype(BF16)
    p = _dot(h, win_ref[...]) + bin_ref[...]

    a = p[:, :c] * jax.nn.sigmoid(p[:, c:2 * c])
    if carry_hist:
        @pl.when(pl.program_id(1) == 0)
        def _():
            aext_ref[:, :HIST_PAD, :] = jnp.zeros((ns, HIST_PAD, c), F32)
    else:
        aext_ref[:, HIST_PAD - HIST:HIST_PAD, :] = hist_ref[...]
    aext_ref[:, HIST_PAD:, :] = a.reshape(ns, t, c)

    off = HIST_PAD - HIST
    n_cc = t // conv_rows
    lag = lag_ref[...]
    lab = lab_ref[...]
    cb = cb_ref[...]

    def finish(acc):
        y = _layer_norm(acc + cb, lag, lab)
        return jax.nn.silu(y).astype(BF16)

    def conv_direct(s):
        acc = jnp.zeros((conv_rows, c), F32)
        for k in range(CONV_W):
            acc = acc + aext_ref[s, off + k:off + k + conv_rows, :] * cw_ref[k:k + 1, :]
        return finish(acc)

    def conv_phased(s, r0):
        acc = jnp.zeros((conv_rows, c), F32)
        for k in range(CONV_W):
            r = (off + k) % SUBLANES
            q8 = pl.multiple_of(r0 + (off + k - r), SUBLANES)
            if r == 0:
                tap = aext_ref[s, pl.ds(q8, conv_rows), :]
            else:
                tap = tapwin_ref[r - 1, pl.ds(q8, conv_rows), :]
            acc = acc + tap * cw_ref[k:k + 1, :]
        return finish(acc)

    for s in range(ns):
        if n_cc == 1:
            ya_ref[s * t:(s + 1) * t, :] = conv_direct(s)
        else:
            span = t + HIST_PAD - SUBLANES
            for r in range(1, SUBLANES):
                tapwin_ref[r - 1] = aext_ref[s, r:r + span, :]

            def body(i, carry, s=s):
                r0 = pl.multiple_of(i * conv_rows, conv_rows)
                ya_ref[pl.ds(s * t + r0, conv_rows), :] = conv_phased(s, r0)
                return carry
            lax.fori_loop(0, n_cc, body, 0, unroll=CONV_UNROLL)

    conv_ref[...] = aext_ref[:, t + HIST_PAD - HIST:t + HIST_PAD, :]
    if carry_hist:
        aext_ref[:, :HIST_PAD, :] = aext_ref[:, t:t + HIST_PAD, :]

    z = jax.nn.gelu(p[:, 2 * c:])
    u = z[:, :c]
    v = _layer_norm(z[:, c:], lvg_ref[...], lvb_ref[...])
    if v_ref is not None:
        v_ref[...] = v
    ri = lax.broadcasted_iota(jnp.int32, (GMLP_CHUNK, GMLP_CHUNK), 0)
    ci = lax.broadcasted_iota(jnp.int32, (GMLP_CHUNK, GMLP_CHUNK), 1)
    tril = ri >= ci
    low_half = ci < (LANES // 2)
    w_heads = [jnp.where(tril, wsp_ref[hh], 0.0).astype(BF16) for hh in range(GMLP_HEADS)]
    for blk in range(rows // GMLP_CHUNK):
        r = slice(blk * GMLP_CHUNK, (blk + 1) * GMLP_CHUNK)
        for j in range(c // LANES):
            l = slice(j * LANES, (j + 1) * LANES)
            vp = v[r, l]
            lo = jnp.where(low_half, vp, 0.0).astype(BF16)
            hi = jnp.where(low_half, 0.0, vp).astype(BF16)
            sp = _dot(w_heads[2 * j], lo) + _dot(w_heads[2 * j + 1], hi)
            yb_ref[r, l] = (u[r, l] * (sp + bsp_ref[:, l])).astype(BF16)

    out = _dot(ya_ref[...], woa_ref[...]) + _dot(yb_ref[...], wob_ref[...]) + bout_ref[...]
    xo_ref[...] = x + mod_ref[2] * out


def _mixer(x, mod, mod_spec, g1, wts, *, ns, t, n_outer, n_inner, hist=None):
    (w_in, b_in, cw, cb, lag, lab, lvg, lvb, wsp, bsp, woa, wob, b_out) = wts
    d = x.shape[1]
    c = cw.shape[1]
    rows = ns * t
    carry_hist = hist is None
    tile = lambda o, i: (o * n_inner + i, 0)
    in_specs = [
        pl.BlockSpec((rows, d), tile),
        mod_spec,
        _const_spec((1, d)),
        _const_spec(w_in.shape), _const_spec(b_in.shape),
        _const_spec(cw.shape), _const_spec(cb.shape),
        _const_spec(lag.shape), _const_spec(lab.shape),
        _const_spec(lvg.shape), _const_spec(lvb.shape),
        _const_spec(wsp.shape), _const_spec(bsp.shape),
        _const_spec(woa.shape), _const_spec(wob.shape), _const_spec(b_out.shape),
    ]
    args = [x, mod, g1, w_in, b_in, cw, cb, lag, lab, lvg, lvb, wsp, bsp, woa, wob, b_out]
    n_tiles = n_outer * n_inner
    out_shape = [jax.ShapeDtypeStruct(x.shape, F32)]
    out_specs = [pl.BlockSpec((rows, d), tile)]
    if carry_hist:
        out_shape.append(jax.ShapeDtypeStruct((n_outer * ns, HIST, c), F32))
        out_specs.append(pl.BlockSpec((ns, HIST, c), lambda o, i: (o, 0, 0)))
    else:
        in_specs.append(pl.BlockSpec((ns, HIST, c), lambda o, i: (o * n_inner + i, 0, 0)))
        args.append(hist)
        out_shape.append(jax.ShapeDtypeStruct((n_tiles * ns, HIST, c), F32))
        out_specs.append(pl.BlockSpec((ns, HIST, c), lambda o, i: (o * n_inner + i, 0, 0)))
        out_shape.append(jax.ShapeDtypeStruct((x.shape[0], c), F32))
        out_specs.append(pl.BlockSpec((rows, c), tile))
    conv_rows = min(t, CONV_ROWS)
    return pl.pallas_call(
        functools.partial(_mixer_kernel, ns=ns, t=t, carry_hist=carry_hist, conv_rows=conv_rows),
        grid=(n_outer, n_inner),
        in_specs=in_specs,
        out_specs=out_specs,
        out_shape=out_shape,
        scratch_shapes=[
            pltpu.VMEM((ns, t + HIST_PAD, c), F32),
            pltpu.VMEM((rows, c), BF16),
            pltpu.VMEM((rows, c), BF16),
            pltpu.VMEM((SUBLANES - 1, t + HIST_PAD - SUBLANES, c), F32),
        ],
        compiler_params=_cparams("arbitrary", "arbitrary"),
        name="mixer_prompt" if carry_hist else "mixer_sample",
    )(*args)


def _ffn_kernel(x_ref, mod_ref, g_ref, wgu_ref, wd_ref, o_ref, *, chunk):
    x = x_ref[...]
    h = _modulate(x, g_ref[...], mod_ref[0], mod_ref[1]).astype(BF16)
    ff = wd_ref.shape[0]
    acc = None
    for c0 in range(0, ff, chunk):
        c1 = min(c0 + chunk, ff)
        gate = _dot(h, wgu_ref[:, c0:c1])
        up = _dot(h, wgu_ref[:, ff + c0:ff + c1])
        act = (jax.nn.silu(gate) * up).astype(BF16)
        part = _dot(act, wd_ref[c0:c1, :])
        acc = part if acc is None else acc + part
    o_ref[...] = x + mod_ref[2] * acc


def _ffn(x, mod, mod_spec, g2, wgu, wd, *, tm, chunk):
    n, d = x.shape
    return pl.pallas_call(
        functools.partial(_ffn_kernel, chunk=chunk),
        grid=(n // tm,),
        in_specs=[
            pl.BlockSpec((tm, d), lambda i: (i, 0)),
            mod_spec,
            _const_spec((1, d)),
            _const_spec(wgu.shape), _const_spec(wd.shape),
        ],
        out_specs=pl.BlockSpec((tm, d), lambda i: (i, 0)),
        out_shape=jax.ShapeDtypeStruct((n, d), F32),
        compiler_params=_cparams("parallel"),
        name="ffn",
    )(x, mod, g2, wgu, wd)


def _mla_latents(x_ref, mod_ref, g_ref, wdc_ref, gq_ref, gkv_ref, tc, ts):
    x = x_ref[...]
    h = _modulate(x, g_ref[...], mod_ref[0], mod_ref[1]).astype(BF16)
    p = _dot(h, wdc_ref[...])
    cq = p[:, :Q_LORA]
    cq = cq * lax.rsqrt(jnp.mean(cq * cq, axis=-1, keepdims=True) + EPS) * gq_ref[...]
    ckv = p[:, Q_LORA:Q_LORA + KV_LORA]
    ckv = ckv * lax.rsqrt(jnp.mean(ckv * ckv, axis=-1, keepdims=True) + EPS) * gkv_ref[...]
    b0 = Q_LORA + KV_LORA
    kpe = p[:, b0:b0 + LANES] * tc + p[:, b0 + LANES:b0 + 2 * LANES] * ts
    return cq.astype(BF16), ckv, kpe


def _mla_prompt_kernel(x_ref, mod_ref, g_ref, wdc_ref, gq_ref, gkv_ref, tc_ref, ts_ref,
                       wqa_ref, wqb_ref, wuk_ref, wuvt_ref,
                       ckv_ref, kpe_ref, q_ref, k_ref, vt_ref):
    tc = tc_ref[...]
    ts = ts_ref[...]
    cq, ckv, kpe = _mla_latents(x_ref, mod_ref, g_ref, wdc_ref, gq_ref, gkv_ref, tc, ts)
    ckv_ref[...] = ckv
    kpe_ref[...] = kpe[:, :QK_ROPE]
    qa = _dot(cq, wqa_ref[...])
    qb = _dot(cq, wqb_ref[...])
    ckv_b = ckv.astype(BF16)
    kn = _dot(ckv_b, wuk_ref[...])
    vt = lax.dot_general(wuvt_ref[...], ckv_b, _NT, preferred_element_type=F32).astype(BF16)
    tm = vt.shape[1]
    pad_rows = V_AUG - V_HD
    ones_row = jnp.where(lax.broadcasted_iota(jnp.int32, (pad_rows, tm), 0) == 0, 1.0, 0.0).astype(BF16)
    for hh in range(MLA_HEADS):
        vt_ref[hh * V_AUG:hh * V_AUG + V_HD, :] = vt[hh * V_HD:(hh + 1) * V_HD, :]
        vt_ref[hh * V_AUG + V_HD:(hh + 1) * V_AUG, :] = ones_row
    kpe_b = kpe.astype(BF16)
    for hh in range(MLA_HEADS):
        base = hh * QK_CAT
        q_ref[:, base:base + QK_NOPE] = qa[:, base:base + QK_NOPE].astype(BF16)
        q_ref[:, base + QK_NOPE:base + QK_CAT] = (
            qa[:, base + QK_NOPE:base + QK_CAT] * tc
            + qb[:, hh * LANES:(hh + 1) * LANES] * ts).astype(BF16)
        k_ref[:, base:base + QK_NOPE] = kn[:, hh * QK_NOPE:(hh + 1) * QK_NOPE].astype(BF16)
        k_ref[:, base + QK_NOPE:base + QK_CAT] = kpe_b


def _mla_prompt(x, mod, mod_spec, g1, wts, tc, ts, *, tm, tiles_per_seq):
    wdc, gq, gkv, wqa, wqb, wuk, wuvt = wts
    n, d = x.shape
    hq = MLA_HEADS * QK_CAT
    hv = MLA_HEADS * V_AUG
    nb = n // (tm * tiles_per_seq)
    row = lambda i: (i, 0)
    pos = lambda i: (i % tiles_per_seq, 0)
    return pl.pallas_call(
        _mla_prompt_kernel,
        grid=(n // tm,),
        in_specs=[
            pl.BlockSpec((tm, d), row), mod_spec, _const_spec((1, d)),
            _const_spec(wdc.shape), _const_spec(gq.shape), _const_spec(gkv.shape),
            pl.BlockSpec((tm, LANES), pos), pl.BlockSpec((tm, LANES), pos),
            _const_spec(wqa.shape), _const_spec(wqb.shape),
            _const_spec(wuk.shape), _const_spec(wuvt.shape),
        ],
        out_specs=[
            pl.BlockSpec((tm, KV_LORA), row), pl.BlockSpec((tm, QK_ROPE), row),
            pl.BlockSpec((tm, hq), row), pl.BlockSpec((tm, hq), row),
            pl.BlockSpec((None, hv, tm), lambda i: (i // tiles_per_seq, 0, i % tiles_per_seq)),
        ],
        out_shape=[
            jax.ShapeDtypeStruct((n, KV_LORA), F32), jax.ShapeDtypeStruct((n, QK_ROPE), F32),
            jax.ShapeDtypeStruct((n, hq), BF16), jax.ShapeDtypeStruct((n, hq), BF16),
            jax.ShapeDtypeStruct((nb, hv, tm * tiles_per_seq), BF16),
        ],
        compiler_params=_cparams("parallel"),
        name="mla_proj_prompt",
    )(x, mod, g1, wdc, gq, gkv, tc, ts, wqa, wqb, wuk, wuvt)


def _mla_sample_kernel(x_ref, mod_ref, g_ref, wdc_ref, gq_ref, gkv_ref, tc_ref, ts_ref,
                       wqn_ref, wqp_ref, wqs_ref, wukt_ref,
                       ckv_ref, kpe_ref, qlat_ref, qpe_ref):
    tc = tc_ref[...]
    ts = ts_ref[...]
    cq, ckv, kpe = _mla_latents(x_ref, mod_ref, g_ref, wdc_ref, gq_ref, gkv_ref, tc, ts)
    ckv_ref[...] = ckv
    kpe_ref[...] = kpe[:, :QK_ROPE]
    qn = _dot(cq, wqn_ref[...]).astype(BF16)
    qp = _dot(cq, wqp_ref[...])
    qs = _dot(cq, wqs_ref[...])
    for j in range(MLA_HEADS * QK_ROPE // LANES):
        l = slice(j * LANES, (j + 1) * LANES)
        qpe_ref[:, l] = (qp[:, l] * tc + qs[:, l] * ts).astype(BF16)
    for hh in range(MLA_HEADS):
        qlat_ref[:, hh * KV_LORA:(hh + 1) * KV_LORA] = _dot(
            qn[:, hh * QK_NOPE:(hh + 1) * QK_NOPE], wukt_ref[hh]).astype(BF16)


def _mla_sample(x, mod, mod_spec, g1, wts, tc, ts, *, tm):
    wdc, gq, gkv, wqn, wqp, wqs, wukt = wts
    n, d = x.shape
    row = lambda i: (i, 0)
    return pl.pallas_call(
        _mla_sample_kernel,
        grid=(n // tm,),
        in_specs=[
            pl.BlockSpec((tm, d), row), mod_spec, _const_spec((1, d)),
            _const_spec(wdc.shape), _const_spec(gq.shape), _const_spec(gkv.shape),
            pl.BlockSpec((tm, LANES), row), pl.BlockSpec((tm, LANES), row),
            _const_spec(wqn.shape), _const_spec(wqp.shape), _const_spec(wqs.shape),
            _const_spec(wukt.shape),
        ],
        out_specs=[
            pl.BlockSpec((tm, KV_LORA), row), pl.BlockSpec((tm, QK_ROPE), row),
            pl.BlockSpec((tm, MLA_HEADS * KV_LORA), row),
            pl.BlockSpec((tm, MLA_HEADS * QK_ROPE), row),
        ],
        out_shape=[
            jax.ShapeDtypeStruct((n, KV_LORA), F32), jax.ShapeDtypeStruct((n, QK_ROPE), F32),
            jax.ShapeDtypeStruct((n, MLA_HEADS * KV_LORA), BF16),
            jax.ShapeDtypeStruct((n, MLA_HEADS * QK_ROPE), BF16),
        ],
        compiler_params=_cparams("parallel"),
        name="mla_proj_sample",
    )(x, mod, g1, wdc, gq, gkv, tc, ts, wqn, wqp, wqs, wukt)


def _attn_prompt_kernel(q_ref, k_ref, vt_ref, o_ref, s_scr, p_scr, acc_scr, *, tq, tk):
    qi = pl.program_id(2)
    n_full = 2 * qi

    def qk_to(slot, t):
        k = k_ref[pl.ds(pl.multiple_of(t * tk, tk), tk), :]
        s = lax.dot_general(k, q_ref[...], _NT, preferred_element_type=F32)
        s_scr[slot] = s
        return jnp.max(s, axis=0, keepdims=True)

    def softmax_from(slot, m, bm):
        m_new = jnp.maximum(m, bm)
        alpha = jnp.exp2(m - m_new)
        p_scr[slot] = jnp.exp2(s_scr[slot] - m_new).astype(BF16)
        return m_new, alpha

    def pv_from(slot, t, alpha):
        vt = vt_ref[:, pl.ds(pl.multiple_of(t * tk, tk), tk)]
        acc_scr[...] = alpha * acc_scr[...] + _dot(vt, p_scr[slot])

    p_scr[1] = jnp.zeros((tk, tq), BF16)
    acc_scr[...] = jnp.zeros((V_AUG, tq), F32)
    m0 = jnp.full((1, tq), NEG_BIG, F32)
    bm0 = qk_to(0, 0)

    def body(i, carry):
        m, bm, a_prev = carry
        t = 2 * i
        bm1 = qk_to(1, t + 1)
        m, a0 = softmax_from(0, m, bm)
        pv_from(1, jnp.maximum(t - 1, 0), a_prev)
        bm2 = qk_to(0, t + 2)
        m, a1 = softmax_from(1, m, bm1)
        pv_from(0, t, a0)
        return m, bm2, a1

    m, bm, a_prev = lax.fori_loop(0, qi, body, (m0, bm0, jnp.ones((1, tq), F32)))
    pv_from(1, jnp.maximum(n_full - 1, 0), a_prev)
    kk = lax.broadcasted_iota(jnp.int32, (tk, tk), 0) // CHUNK
    qq = lax.broadcasted_iota(jnp.int32, (tk, tk), 1) // CHUNK
    vis = qq >= kk
    s_a = jnp.where(vis, s_scr[0, :, :tk], NEG_BIG)
    bm_d = jnp.concatenate([jnp.max(s_a, axis=0, keepdims=True), bm[:, tk:]], axis=1)
    m_new = jnp.maximum(m, bm_d)
    alpha = jnp.exp2(m - m_new)
    p_a = jnp.exp2(s_a - m_new[:, :tk])
    p_b = jnp.exp2(s_scr[0, :, tk:] - m_new[:, tk:])
    p_scr[0, :, :tk] = p_a.astype(BF16)
    p_scr[0, :, tk:] = p_b.astype(BF16)
    pv_from(0, n_full, alpha)
    k1 = k_ref[pl.ds(pl.multiple_of((n_full + 1) * tk, tk), tk), :]
    s1 = lax.dot_general(k1, q_ref[tk:, :], _NT, preferred_element_type=F32)
    s1 = jnp.where(vis, s1, NEG_BIG)
    m_b = m_new[:, tk:]
    m_b2 = jnp.maximum(m_b, jnp.max(s1, axis=0, keepdims=True))
    a1 = jnp.exp2(m_b - m_b2)
    p1 = jnp.exp2(s1 - m_b2)
    vt1 = vt_ref[:, pl.ds(pl.multiple_of((n_full + 1) * tk, tk), tk)]
    acc_b = a1 * acc_scr[:, tk:] + _dot(vt1, p1.astype(BF16))
    o_ref[:tk, :] = (acc_scr[:V_HD, :tk] / acc_scr[V_HD:V_HD + 1, :tk]).T.astype(o_ref.dtype)
    o_ref[tk:, :] = (acc_b[:V_HD] / acc_b[V_HD:V_HD + 1]).T.astype(o_ref.dtype)


def _attn_prompt(q, k, vt, *, tq):
    b, s, _ = q.shape
    tk = tq // 2
    return pl.pallas_call(
        functools.partial(_attn_prompt_kernel, tq=tq, tk=tk),
        grid=(b, MLA_HEADS, s // tq),
        in_specs=[
            pl.BlockSpec((None, tq, QK_CAT), lambda bi, hi, qi: (bi, qi, hi)),
            pl.BlockSpec((None, s, QK_CAT), lambda bi, hi, qi: (bi, 0, hi)),
            pl.BlockSpec((None, V_AUG, s), lambda bi, hi, qi: (bi, hi, 0)),
        ],
        out_specs=pl.BlockSpec((None, tq, V_HD), lambda bi, hi, qi: (bi, qi, hi)),
        out_shape=jax.ShapeDtypeStruct((b, s, MLA_HEADS * V_HD), BF16),
        scratch_shapes=[pltpu.VMEM((2, tk, tq), F32), pltpu.VMEM((2, tk, tq), BF16),
                        pltpu.VMEM((V_AUG, tq), F32)],
        compiler_params=_cparams("parallel", "parallel", "arbitrary"),
        name="attn_prompt",
    )(q, k, vt)


def _attn_sample_kernel(ql_ref, qp_ref, cc_ref, cpt_ref, nc_ref, np_ref, o_ref):
    for g in range(ql_ref.shape[0]):
        ql = ql_ref[g]
        qp = qp_ref[g]
        cc = cc_ref[g].astype(BF16)
        cpt = cpt_ref[g].astype(BF16)
        nc = nc_ref[g].astype(BF16)
        npe = np_ref[g].astype(BF16)
        s_c = lax.dot_general(ql, cc, _NT, preferred_element_type=F32) + _dot(qp, cpt)
        s_n = (lax.dot_general(ql, nc, _NT, preferred_element_type=F32)
               + lax.dot_general(qp, npe, _NT, preferred_element_type=F32))
        m = jnp.maximum(jnp.max(s_c, axis=-1, keepdims=True), jnp.max(s_n, axis=-1, keepdims=True))
        p_c = jnp.exp2(s_c - m)
        p_n = jnp.exp2(s_n - m)
        l = jnp.sum(p_c, axis=-1, keepdims=True) + jnp.sum(p_n, axis=-1, keepdims=True)
        o = _dot(p_c.astype(BF16), cc) + _dot(p_n.astype(BF16), nc)
        o_ref[g] = (o / l).astype(o_ref.dtype)


def _attn_sample(qlat, qpe, cache_ckv, cache_kpe_t, ckv_new, kpe_new, *, streams=SAMPLE_STREAMS):
    nb, past, _ = cache_ckv.shape
    r = qlat.shape[1]
    t = ckv_new.shape[1]
    blk = lambda shape: pl.BlockSpec((streams,) + shape, lambda i: (i, 0, 0))
    return pl.pallas_call(
        _attn_sample_kernel,
        grid=(nb // streams,),
        in_specs=[blk((r, KV_LORA)), blk((r, QK_ROPE)), blk((past, KV_LORA)),
                  blk((QK_ROPE, past)), blk((t, KV_LORA)), blk((t, QK_ROPE))],
        out_specs=blk((r, KV_LORA)),
        out_shape=jax.ShapeDtypeStruct((nb, r, KV_LORA), BF16),
        compiler_params=_cparams("parallel"),
        name="attn_sample",
    )(qlat, qpe, cache_ckv, cache_kpe_t, ckv_new, kpe_new)


def _route_tail(xm, mod_ref, g_ref, wrc_ref, br_ref, xo_ref, h_ref, route_ref, routet_ref, cnt_ref):
    tm = xm.shape[0]
    xo_ref[...] = xm
    hf = _modulate(xm, g_ref[...], mod_ref[3], mod_ref[4])
    hb = hf.astype(BF16)
    h_ref[...] = hb
    h_lo = (hf - hb.astype(F32)).astype(BF16)
    both = _dot(hb, wrc_ref[...])
    logits = (both[:, :LANES] + both[:, LANES:] + _dot(h_lo, wrc_ref[:, :LANES])) + br_ref[...]
    lane = lax.broadcasted_iota(jnp.int32, (tm, LANES), 1)
    m1 = jnp.max(logits, axis=-1, keepdims=True)
    i1 = jnp.min(jnp.where(logits == m1, lane, LANES), axis=-1, keepdims=True)
    rest = jnp.where(lane == i1, -jnp.inf, logits)
    m2 = jnp.max(rest, axis=-1, keepdims=True)
    i2 = jnp.min(jnp.where(rest == m2, lane, LANES), axis=-1, keepdims=True)
    e2 = jnp.exp(m2 - m1)
    den = 1.0 + e2
    sel1 = lane == i1
    sel2 = lane == i2
    onehot = jnp.where(sel1, 1.0, jnp.where(sel2, 1.0, 0.0))
    ri = lax.broadcasted_iota(jnp.int32, (tm, tm), 0)
    ci = lax.broadcasted_iota(jnp.int32, (tm, tm), 1)
    before = jnp.where(ri > ci, 1.0, 0.0).astype(BF16)
    rank = _dot(before, onehot.astype(BF16))
    cnt = jnp.sum(onehot, axis=0, keepdims=True)
    nblk = jnp.floor((cnt + (MOE_PAD - 1)) * (1.0 / MOE_PAD))
    r8 = lax.broadcasted_iota(jnp.int32, (LANES, LANES), 0)
    c8 = lax.broadcasted_iota(jnp.int32, (LANES, LANES), 1)
    upper = jnp.where(r8 < c8, 1.0, 0.0).astype(BF16)
    first_blk = _dot(jnp.broadcast_to(nblk, (8, LANES)).astype(BF16), upper)[0:1]
    slot = first_blk * MOE_PAD + rank
    d1 = jnp.sum(jnp.where(sel1, slot, 0.0), axis=-1, keepdims=True)
    d2 = jnp.sum(jnp.where(sel2, slot, 0.0), axis=-1, keepdims=True)
    route = jnp.where(lane == 0, d1, jnp.where(lane == 1, d2, jnp.where(
        lane == 2, 1.0 / den, jnp.where(lane == 3, e2 / den, 0.0))))
    route_ref[...] = route
    routet_ref[...] = route.T[:8]
    cnt_ref[...] = jnp.broadcast_to(cnt, (8, LANES))


def _oproj_route_kernel(x_ref, mod_ref, o_ref, wo_ref, g_ref, wrc_ref, br_ref,
                        xo_ref, h_ref, route_ref, routet_ref, cnt_ref):
    xm = x_ref[...] + mod_ref[2] * _dot(o_ref[...], wo_ref[...])
    _route_tail(xm, mod_ref, g_ref, wrc_ref, br_ref, xo_ref, h_ref, route_ref, routet_ref, cnt_ref)


def _oproj_latent_route_kernel(x_ref, mod_ref, o_ref, wuv_ref, wo_ref, g_ref, wrc_ref,
                               br_ref, xo_ref, h_ref, route_ref, routet_ref, cnt_ref):
    acc = None
    for hh in range(MLA_HEADS):
        oh = _dot(o_ref[:, hh * KV_LORA:(hh + 1) * KV_LORA], wuv_ref[hh]).astype(BF16)
        part = _dot(oh, wo_ref[hh * V_HD:(hh + 1) * V_HD, :])
        acc = part if acc is None else acc + part
    xm = x_ref[...] + mod_ref[2] * acc
    _route_tail(xm, mod_ref, g_ref, wrc_ref, br_ref, xo_ref, h_ref, route_ref, routet_ref, cnt_ref)


def _oproj_route(x, mod, mod_spec, o, wo, g2, wrc, br, wuv=None, *, tm):
    n, d = x.shape
    row = lambda i: (i, 0)
    in_specs = [pl.BlockSpec((tm, d), row), mod_spec, pl.BlockSpec((tm, o.shape[1]), row)]
    args = [x, mod, o]
    if wuv is not None:
        in_specs.append(_const_spec(wuv.shape))
        args.append(wuv)
    in_specs += [_const_spec(wo.shape), _const_spec((1, d)), _const_spec(wrc.shape),
                 _const_spec(br.shape)]
    args += [wo, g2, wrc, br]
    return pl.pallas_call(
        _oproj_route_kernel if wuv is None else _oproj_latent_route_kernel,
        grid=(n // tm,),
        in_specs=in_specs,
        out_specs=[pl.BlockSpec((tm, d), row), pl.BlockSpec((tm, d), row),
                   pl.BlockSpec((tm, LANES), row), pl.BlockSpec((None, 8, tm), lambda i: (i, 0, 0)),
                   pl.BlockSpec((None, 8, LANES), lambda i: (i, 0, 0))],
        out_shape=[jax.ShapeDtypeStruct((n, d), F32), jax.ShapeDtypeStruct((n, d), BF16),
                   jax.ShapeDtypeStruct((n, LANES), F32),
                   jax.ShapeDtypeStruct((n // tm, 8, tm), F32),
                   jax.ShapeDtypeStruct((n // tm, 8, LANES), F32)],
        compiler_params=_cparams("parallel"),
        name="oproj_route" if wuv is None else "oproj_latent_route",
    )(*args)


def _moe_routed_kernel(row0_ref, npad_ref, h_ref, x_ref, mod_ref, rc_ref, rr_ref,
                       wgu_ref, wd_ref, fg_ref, y_ref, acc_ref):
    i = pl.program_id(0)
    e = pl.program_id(1)
    n_exp = pl.num_programs(1)
    tm = h_ref.shape[0]
    start = row0_ref[i * n_exp + e]
    n_pad = npad_ref[i * n_exp + e]
    per = MOE_ROWS // MOE_PAD
    has_head = n_pad % per

    @pl.when(e == 0)
    def _():
        acc_ref[...] = jnp.zeros(acc_ref.shape, F32)

    def block(rows, first_slot):
        base = first_slot.astype(F32)
        rr = rr_ref[...]
        srow = lax.broadcasted_iota(jnp.int32, (rows, tm), 0).astype(F32) + base
        hit1 = rr[0:1, :] == srow
        hit2 = rr[1:2, :] == srow
        sel = jnp.where(hit1, 1.0, jnp.where(hit2, 1.0, 0.0)).astype(BF16)
        gate = jnp.sum(jnp.where(hit1, rr[2:3, :], jnp.where(hit2, rr[3:4, :], 0.0)),
                       axis=-1, keepdims=True)
        hb = _dot(sel, h_ref[...]).astype(BF16)
        gu = _dot(hb, wgu_ref[...])
        ff = gu.shape[1] // 2
        act = (jax.nn.silu(gu[:, :ff]) * gu[:, ff:]).astype(BF16)
        yb = (_dot(act, wd_ref[...]) * gate).astype(BF16)
        rc = rc_ref[...]
        scol = lax.broadcasted_iota(jnp.int32, (tm, rows), 1).astype(F32) + base
        sel_t = jnp.where(rc[:, 0:1] == scol, 1.0,
                          jnp.where(rc[:, 1:2] == scol, 1.0, 0.0)).astype(BF16)
        acc_ref[...] += _dot(sel_t, yb)

    for units in range(1, per):
        @pl.when(has_head == units)
        def _(units=units):
            block(units * MOE_PAD, start)

    def body(b, carry):
        block(MOE_ROWS, start + has_head * MOE_PAD + b * MOE_ROWS)
        return carry

    lax.fori_loop(0, n_pad // per, body, 0)

    @pl.when(e == n_exp - 1)
    def _():
        xn = x_ref[...] + mod_ref[5] * acc_ref[...]
        y_ref[...] = xn * lax.rsqrt(jnp.mean(xn * xn, axis=-1, keepdims=True) + EPS) * fg_ref[...]


def _moe_routed(h, x, mod, mod_spec2, route, route_row, cnt, wgu, wd, fg, *, tm):
    n, d = x.shape
    ne, ff, _ = wd.shape
    n_tiles = n // tm
    n_pad = jnp.ceil(cnt[:, 0, :ne] / MOE_PAD).astype(jnp.int32)
    row0 = (jnp.cumsum(n_pad, axis=1) - n_pad) * MOE_PAD
    wspec = lambda shape: pl.BlockSpec((None,) + shape, lambda i, e, r0, npd: (e, 0, 0))
    return pl.pallas_call(
        _moe_routed_kernel,
        grid_spec=pltpu.PrefetchScalarGridSpec(
            num_scalar_prefetch=2,
            grid=(n_tiles, ne),
            in_specs=[
                pl.BlockSpec((tm, d), lambda i, e, r0, npd: (i, 0)),
                pl.BlockSpec((tm, d), lambda i, e, r0, npd: (i, 0)), mod_spec2,
                pl.BlockSpec((tm, LANES), lambda i, e, r0, npd: (i, 0)),
                pl.BlockSpec((None, 8, tm), lambda i, e, r0, npd: (i, 0, 0)),
                wspec((d, 2 * ff)), wspec((ff, d)),
                pl.BlockSpec((1, d), lambda i, e, r0, npd: (0, 0)),
            ],
            out_specs=pl.BlockSpec((tm, d), lambda i, e, r0, npd: (i, 0)),
            scratch_shapes=[pltpu.VMEM((tm, d), F32)],
        ),
        out_shape=jax.ShapeDtypeStruct((n, d), F32),
        compiler_params=_cparams("parallel", "arbitrary"),
        name="moe_routed",
    )(row0.reshape(-1), n_pad.reshape(-1), h, x, mod, route, route_row, wgu, wd, fg)


def _rope_tables(pos):
    half = QK_ROPE // 2
    inv = jnp.exp(-math.log(ROPE_BASE) * jnp.arange(half, dtype=F32) / half)
    ang = pos.astype(F32)[:, None] * inv[None, :]
    cos, sin = jnp.cos(ang), jnp.sin(ang)
    reps = LANES // QK_ROPE
    tc = jnp.tile(jnp.concatenate([cos, cos], axis=-1), (1, reps))
    ts = jnp.tile(jnp.concatenate([-sin, sin], axis=-1), (1, reps))
    return tc, ts


def _swap_halves(w):
    half = w.shape[-1] // 2
    return jnp.concatenate([w[..., half:], w[..., :half]], axis=-1)


def kernel(x_prompt, x_sample, c_prompt, c_sample, state_conv, cache_ckv, cache_kpe, norm1_g, norm2_g, w_ada, b_ada, w_in_ab, b_in_ab, conv_w, conv_b, ln_conv_g, ln_conv_b, ln_v_g, ln_v_b, w_spatial, b_spatial, w_out_ab, b_out_ab, w_ffn_gu, w_ffn_down, w_dc, g_q, g_kv, w_uq, w_uk, w_uv, w_o, w_router, b_router, w_exp_gu, w_exp_down, final_g):
    nb, seq, d = x_prompt.shape
    ns_b, t_s, _ = x_sample.shape
    past = cache_ckv.shape[2]
    c_ch = conv_w.shape[-1]
    n_p = nb * seq
    n_s = ns_b * t_s
    tm, tq, tmoe, tm_mla, tm_ffn = TILE_MIXER, TILE_Q, TILE_MOE, TILE_MLA, TILE_FFN
    assert all(seq % tile == 0 for tile in (tm, tq, tmoe, tm_mla, tm_ffn)), "prompt length must tile evenly"
    assert GMLP_CHUNK % t_s == 0 and n_s % GMLP_CHUNK == 0, "sample frames must pack into 128-row mixer tiles"
    tiles_per_seq = seq // tm
    s_group = GMLP_CHUNK // t_s

    xp = x_prompt.reshape(n_p, d)
    xs = x_sample.reshape(n_s, d)

    mods = _adaln(jnp.concatenate([c_prompt, c_sample], axis=0), w_ada, b_ada)
    depth = w_ada.shape[0]
    mods = mods.reshape(depth, nb + ns_b, 6, d)

    def prompt_mod(l, k0):
        return mods[l, :nb, k0:k0 + 3][:, :, None, :]

    def sample_mod(l, k0):
        m = mods[l, nb:, k0:k0 + 3]
        m = jnp.broadcast_to(m[:, None], (ns_b, t_s, 3, d)).reshape(n_s, 3, d)
        return m.transpose(1, 0, 2)

    def pspec(tiles_per_b):
        return pl.BlockSpec((None, 3, 1, d), lambda i: (i // tiles_per_b, 0, 0, 0))

    def sspec(rows):
        return pl.BlockSpec((3, rows, d), lambda i: (0, i, 0))

    g1 = norm1_g[0][None]
    g2 = norm2_g[0][None]
    w_in = w_in_ab[0].astype(BF16)
    woa = w_out_ab[0][:c_ch].astype(BF16)
    wob = w_out_ab[0][c_ch:].astype(BF16)
    hd = c_ch // GMLP_HEADS
    bsp_p = jnp.repeat(b_spatial[0].T, hd, axis=1)
    wsp_p = w_spatial[0]
    eye = jnp.eye(s_group, dtype=F32)
    wsp_s = jnp.einsum('ab,hts->hatbs', eye, w_spatial[0][:, :t_s, :t_s]).reshape(
        GMLP_HEADS, GMLP_CHUNK, GMLP_CHUNK)
    bsp_s = jnp.tile(jnp.repeat(b_spatial[0][:, :t_s].T, hd, axis=1), (s_group, 1))
    common = (w_in, b_in_ab[0][None], conv_w[0], conv_b[0][None], ln_conv_g[0][None],
              ln_conv_b[0][None], ln_v_g[0][None], ln_v_b[0][None])
    tail = (woa, wob, b_out_ab[0][None])

    pm = prompt_mod(0, 0)
    xp, conv_p = _mixer(
        xp, pm, pl.BlockSpec((None, 3, 1, d), lambda o, i: (o, 0, 0, 0)), g1,
        common + (wsp_p, bsp_p) + tail, ns=1, t=tm, n_outer=nb, n_inner=tiles_per_seq)
    sm = sample_mod(0, 0)
    n_st = n_s // GMLP_CHUNK
    xs, conv_s, gv_s = _mixer(
        xs, sm, pl.BlockSpec((3, GMLP_CHUNK, d), lambda o, i: (0, o * n_st + i, 0)), g1,
        common + (wsp_s, bsp_s) + tail, ns=s_group, t=t_s, n_outer=1, n_inner=n_st,
        hist=state_conv[0])

    ff = w_ffn_gu.shape[-1] // 2
    wgu = w_ffn_gu[0].astype(BF16)
    wd = w_ffn_down[0].astype(BF16)
    xp = _ffn(xp, prompt_mod(0, 3), pspec(seq // tm_ffn), g2, wgu, wd, tm=tm_ffn, chunk=FFN_CHUNK)
    xs = _ffn(xs, sample_mod(0, 3), sspec(n_s), g2, wgu, wd, tm=n_s, chunk=ff // 2)

    g1 = norm1_g[1][None]
    g2 = norm2_g[1][None]
    scale = (QK_NOPE + QK_ROPE) ** -0.5 * math.log2(math.e)
    zeros64 = jnp.zeros((d, QK_ROPE), F32)
    w_kpe = w_dc[0][:, Q_LORA + KV_LORA:]
    wdc = jnp.concatenate([w_dc[0][:, :Q_LORA + KV_LORA], w_kpe, zeros64,
                           _swap_halves(w_kpe), zeros64], axis=1).astype(BF16)
    wq3 = w_uq[0].reshape(Q_LORA, MLA_HEADS, QK_NOPE + QK_ROPE) * scale
    wq_nope = wq3[:, :, :QK_NOPE]
    wq_pe = wq3[:, :, QK_NOPE:]
    wq_sw = _swap_halves(wq_pe)
    zpad = jnp.zeros((Q_LORA, MLA_HEADS, QK_ROPE), F32)
    wqa = jnp.concatenate([wq_nope, wq_pe, zpad], axis=-1).reshape(Q_LORA, -1).astype(BF16)
    wqb = jnp.concatenate([wq_sw, zpad], axis=-1).reshape(Q_LORA, -1).astype(BF16)
    wuk = w_uk[0].astype(BF16)
    wuvt = w_uv[0].T.astype(BF16)
    gq = g_q[0][None]
    gkv = g_kv[0][None]
    wo = w_o[0].astype(BF16)
    ne = w_router.shape[-1]
    wr = jnp.zeros((d, LANES), F32).at[:, :ne].set(w_router[0])
    wrh = wr.astype(BF16)
    wrc = jnp.concatenate([wrh, (wr - wrh.astype(F32)).astype(BF16)], axis=1)
    br = jnp.full((1, LANES), NEG_BIG, F32).at[0, :ne].set(b_router[0])
    wegu = w_exp_gu[0].astype(BF16)
    wed = w_exp_down[0].astype(BF16)
    fg = final_g[None]

    tc_p, ts_p = _rope_tables(jnp.arange(seq))
    pm = prompt_mod(1, 0)
    ckv_p, kpe_p, q_p, k_p, vt_p = _mla_prompt(
        xp, pm, pspec(seq // tm_mla), g1, (wdc, gq, gkv, wqa, wqb, wuk, wuvt),
        tc_p, ts_p, tm=tm_mla, tiles_per_seq=seq // tm_mla)
    o_p = _attn_prompt(q_p.reshape(nb, seq, -1), k_p.reshape(nb, seq, -1), vt_p, tq=tq)
    pm6 = mods[1, :nb][:, :, None, :]
    tiles_moe = seq // tmoe
    xp, hp, route_p, routet_p, cnt_p = _oproj_route(
        xp, pm6, pl.BlockSpec((None, 6, 1, d), lambda i: (i // tiles_moe, 0, 0, 0)),
        o_p.reshape(n_p, -1), wo, g2, wrc, br, tm=tmoe)
    yp = _moe_routed(
        hp, xp, pm6, pl.BlockSpec((None, 6, 1, d), lambda i, j, *_: (i // tiles_moe, 0, 0, 0)),
        route_p, routet_p, cnt_p, wegu, wed, fg, tm=tmoe)

    tc_s, ts_s = _rope_tables(past + jnp.arange(t_s))
    tc_s = jnp.tile(tc_s, (ns_b, 1))
    ts_s = jnp.tile(ts_s, (ns_b, 1))
    wqn = wq_nope.reshape(Q_LORA, -1).astype(BF16)
    wqp = wq_pe.reshape(Q_LORA, -1).astype(BF16)
    wqs = wq_sw.reshape(Q_LORA, -1).astype(BF16)
    wukt = w_uk[0].reshape(KV_LORA, MLA_HEADS, QK_NOPE).transpose(1, 2, 0).astype(BF16)
    wuv_h = w_uv[0].reshape(KV_LORA, MLA_HEADS, V_HD).transpose(1, 0, 2).astype(BF16)
    sm = sample_mod(1, 0)
    ckv_s, kpe_s, qlat, qpe = _mla_sample(
        xs, sm, sspec(n_s), g1, (wdc, gq, gkv, wqn, wqp, wqs, wukt), tc_s, ts_s, tm=n_s)
    o_lat = _attn_sample(
        qlat.reshape(ns_b, t_s * MLA_HEADS, KV_LORA), qpe.reshape(ns_b, t_s * MLA_HEADS, QK_ROPE),
        cache_ckv[0], jnp.swapaxes(cache_kpe[0], 1, 2), ckv_s.reshape(ns_b, t_s, KV_LORA),
        kpe_s.reshape(ns_b, t_s, QK_ROPE))
    sm6 = jnp.broadcast_to(mods[1, nb:][:, None], (ns_b, t_s, 6, d)).reshape(n_s, 6, d).transpose(1, 0, 2)
    xs, hs, route_s, routet_s, cnt_s = _oproj_route(
        xs, sm6, pl.BlockSpec((6, n_s, d), lambda i: (0, i, 0)),
        o_lat.reshape(n_s, MLA_HEADS * KV_LORA), wo, g2, wrc, br, wuv_h, tm=n_s)
    ys = _moe_routed(
        hs, xs, sm6, pl.BlockSpec((6, n_s, d), lambda i, j, *_: (0, i, 0)),
        route_s, routet_s, cnt_s, wegu, wed, fg, tm=n_s)

    return (yp.reshape(nb, seq, d), ys.reshape(ns_b, t_s, d),
            conv_p[None], conv_s[None], gv_s.reshape(ns_b, t_s, c_ch)[None],
            ckv_p.reshape(nb, seq, KV_LORA)[None], kpe_p.reshape(nb, seq, QK_ROPE)[None],
            ckv_s.reshape(ns_b, t_s, KV_LORA)[None], kpe_s.reshape(ns_b, t_s, QK_ROPE)[None])
```
